```python
import functools
import jax, jax.numpy as jnp
from jax import lax
import numpy as np

D_MODEL = 1024
BATCH = 2
SEQ = 8192
DEPTH = 4
DEC_BATCH = 128
DEC_SEQ = 4
PAST_LEN = 8192
PAGE_SIZE = 128

HEAD_DIM = 64
SWA_HEADS = 4
SWA_KV_HEADS = 2
SWA_WINDOW = 128
DIL_HEADS = 4
DIL_PAIRS = ((128, 1), (512, 4), (2048, 16))
DIL_WINDOW_MAX = 2048
MLSTM_HEADS = 4
MLSTM_DK = 128
MLSTM_DV = 128
MLSTM_CHUNK = 64
ROPE_DIM = HEAD_DIM // 4
ROPE_THETA = 500000.0
ATTN_BLOCK = 128
MIX_WIDTH = SWA_HEADS * HEAD_DIM + DIL_HEADS * HEAD_DIM + MLSTM_HEADS * MLSTM_DV
PROJ_SPLITS = (SWA_HEADS * HEAD_DIM, SWA_KV_HEADS * HEAD_DIM, SWA_KV_HEADS * HEAD_DIM,
               DIL_HEADS * HEAD_DIM, DIL_HEADS * HEAD_DIM, DIL_HEADS * HEAD_DIM,
               MLSTM_HEADS * MLSTM_DK, MLSTM_HEADS * MLSTM_DK, MLSTM_HEADS * MLSTM_DV,
               MLSTM_HEADS * MLSTM_DV, MLSTM_HEADS, MLSTM_HEADS)
IN_COLS = 4 * SWA_HEADS * HEAD_DIM + 3 * DIL_HEADS * HEAD_DIM + 2 * MLSTM_HEADS * MLSTM_DK + 2 * MLSTM_HEADS * MLSTM_DV + 2 * MLSTM_HEADS - 2 * (SWA_HEADS - SWA_KV_HEADS) * HEAD_DIM - SWA_HEADS * HEAD_DIM
N_EXPERTS = 16
N_EXPERT_GROUPS = 4
EXPERTS_PER_GROUP = N_EXPERTS // N_EXPERT_GROUPS
TOP_K = 2
D_EXPERT = 512
MOE_BLOCK = 128
LN_EPS = 1e-5
DEEPNORM_ALPHA = (2.0 * DEPTH) ** 0.25
DEEPNORM_BETA = (8.0 * DEPTH) ** -0.25

kernel_name = 'hybrid_hymba_swa_dilated_mlstm_moe_step'

F32 = jnp.float32


def layer_norm(x, g, b):
    xf = x.astype(F32)
    mu = xf.mean(-1, keepdims=True)
    var = jnp.mean(jnp.square(xf - mu), -1, keepdims=True)
    return ((xf - mu) * lax.rsqrt(var + LN_EPS) * g.astype(F32) + b.astype(F32)).astype(x.dtype)


def split_columns(proj):
    offsets, acc = [], 0
    for w in PROJ_SPLITS[:-1]:
        acc += w
        offsets.append(acc)
    return jnp.split(proj, offsets, axis=-1)


def partial_rope(x, pos):
    half = ROPE_DIM // 2
    inv_freq = ROPE_THETA ** (-jnp.arange(half, dtype=F32) / half)
    ang = pos.astype(F32)[:, None] * inv_freq[None, :]
    cos = jnp.cos(ang)[None, :, None, :]
    sin = jnp.sin(ang)[None, :, None, :]
    xf = x.astype(F32)
    x1, x2 = xf[..., :half], xf[..., half:ROPE_DIM]
    out = jnp.concatenate([x1 * cos - x2 * sin, x2 * cos + x1 * sin, xf[..., ROPE_DIM:]], axis=-1)
    return out.astype(x.dtype)


def swa_with_sinks(q, kc, vc, n_prefix, sinks):
    bsz, t_len, n_heads, dh = q.shape
    n_kv = kc.shape[2]
    grp = n_heads // n_kv
    bq = min(ATTN_BLOCK, t_len)
    nb = t_len // bq
    nk = SWA_WINDOW + bq
    blk_start = n_prefix + jnp.arange(nb) * bq
    q_idx = blk_start[:, None] + jnp.arange(bq)[None, :]
    k_idx = blk_start[:, None] - SWA_WINDOW + jnp.arange(nk)[None, :]
    kb = jnp.take(kc, jnp.maximum(k_idx, 0), axis=1)
    vb = jnp.take(vc, jnp.maximum(k_idx, 0), axis=1)
    qb = q.reshape(bsz, nb, bq, n_kv, grp, dh)
    s = jnp.einsum('bnqhgd,bnkhd->bnhgqk', qb, kb).astype(F32) * (dh ** -0.5)
    dist = q_idx[:, :, None] - k_idx[:, None, :]
    mask = (dist >= 0) & (dist <= SWA_WINDOW) & (k_idx[:, None, :] >= 0)
    s = jnp.where(mask[None, :, None, None], s, -jnp.inf)
    sink = sinks.astype(F32).reshape(n_kv, grp)[None, None, :, :, None, None]
    m = jnp.maximum(s.max(-1, keepdims=True), sink)
    p = jnp.exp(s - m)
    p = p / (p.sum(-1, keepdims=True) + jnp.exp(sink - m))
    o = jnp.einsum('bnhgqk,bnkhd->bnqhgd', p, vb.astype(F32))
    return o.reshape(bsz, t_len, n_heads * dh).astype(q.dtype)


def dilated_attention(q, kc, vc, n_prefix):
    bsz, t_len, n_heads, dh = q.shape
    bq = min(ATTN_BLOCK, t_len)
    nb = t_len // bq
    q_blocks = jnp.moveaxis(q.reshape(bsz, nb, bq, n_heads, dh), 1, 0)
    scale = dh ** -0.5

    def one_block(args):
        blk, qb = args
        q_idx = n_prefix + blk * bq + jnp.arange(bq)
        scores, values = [], []
        for window, dil in DIL_PAIRS:
            idx = q_idx[:, None] - dil * jnp.arange(window // dil + 1)[None, :]
            kg = jnp.take(kc, jnp.maximum(idx, 0), axis=1)
            vg = jnp.take(vc, jnp.maximum(idx, 0), axis=1)
            s = jnp.einsum('bqhd,bqkhd->bhqk', qb, kg).astype(F32) * scale
            scores.append(jnp.where((idx >= 0)[None, None], s, -jnp.inf))
            values.append(vg)
        m = functools.reduce(jnp.maximum, [s.max(-1) for s in scores])
        outs, dens = [], []
        for s, vg in zip(scores, values):
            e = jnp.exp(s - m[..., None])
            den = jnp.moveaxis(e.sum(-1), 1, 2)
            o = jnp.einsum('bhqk,bqkhd->bqhd', e, vg.astype(F32)) / den[..., None]
            outs.append(o)
            dens.append(den)
        total = functools.reduce(jnp.add, dens)
        return functools.reduce(jnp.add, [d[..., None] * o for d, o in zip(dens, outs)]) / total[..., None]

    out = lax.map(one_block, (jnp.arange(nb), q_blocks))
    return jnp.moveaxis(out, 0, 1).reshape(bsz, t_len, n_heads * dh).astype(q.dtype)


def mlstm_chunkwise(q, k, v, ig, lf, c0, n0, m0):
    bsz, t_len = q.shape[:2]
    csz = min(MLSTM_CHUNK, t_len)
    nc = t_len // csz

    def chunks(a):
        return jnp.moveaxis(a.astype(F32).reshape((bsz, nc, csz) + a.shape[2:]), 1, 0)

    causal = jnp.tril(jnp.ones((csz, csz), dtype=bool))

    def step(carry, xs):
        c, n, m = carry
        qc, kc, vc, ic, fc = xs
        b = jnp.cumsum(fc, axis=1)
        log_d = b[:, :, None, :] - b[:, None, :, :] + ic[:, None, :, :]
        log_d = jnp.where(causal[None, :, :, None], log_d, -jnp.inf)
        log_inter = b + m[:, None, :]
        m_t = jnp.maximum(log_inter, log_d.max(axis=2))
        w_ts = jnp.exp(log_d - m_t[:, :, None, :]) * jnp.einsum('bthd,bshd->btsh', qc, kc)
        w_inter = jnp.exp(log_inter - m_t)
        num = jnp.einsum('btsh,bshv->bthv', w_ts, vc) + w_inter[..., None] * jnp.einsum('bthd,bhdv->bthv', qc, c)
        den = w_ts.sum(axis=2) + w_inter * jnp.einsum('bthd,bhd->bth', qc, n)
        h = num / jnp.maximum(jnp.abs(den), jnp.exp(-m_t))[..., None]
        b_last = b[:, -1, :]
        log_w = b_last[:, None, :] - b + ic
        m_new = jnp.maximum(b_last + m, log_w.max(axis=1))
        w_s = jnp.exp(log_w - m_new[:, None, :])
        decay = jnp.exp(b_last + m - m_new)
        c_new = decay[..., None, None] * c + jnp.einsum('bsh,bshd,bshv->bhdv', w_s, kc, vc)
        n_new = decay[..., None] * n + jnp.einsum('bsh,bshd->bhd', w_s, kc)
        return (c_new, n_new, m_new), h

    (c, n, m), h = lax.scan(step, (c0.astype(F32), n0.astype(F32), m0.astype(F32)),
                            (chunks(q), chunks(k), chunks(v), chunks(ig), chunks(lf)))
    h = jnp.moveaxis(h, 0, 1).reshape(bsz, t_len, q.shape[2], v.shape[-1])
    return h, c, n, m


def head_layer_norm(h, g):
    mu = h.mean(-1, keepdims=True)
    var = jnp.mean(jnp.square(h - mu), -1, keepdims=True)
    hn = (h - mu) * lax.rsqrt(var + LN_EPS)
    return hn.reshape(h.shape[0], h.shape[1], -1) * g.astype(F32)


def moe_ffn(x, router_w, router_b, w_gate, w_up, w_down):
    bsz, t_len, d = x.shape
    xt = x.reshape(-1, d)
    n_tok = xt.shape[0]
    logits = xt.astype(F32) @ router_w.astype(F32) + router_b.astype(F32)
    aff = jax.nn.softmax(logits, axis=-1).reshape(n_tok, N_EXPERT_GROUPS, EXPERTS_PER_GROUP)
    group_score = lax.top_k(aff, TOP_K)[0].sum(-1)
    g_sel = jnp.argmax(group_score, axis=-1)
    sel = jnp.take_along_axis(aff, g_sel[:, None, None], axis=1)[:, 0]
    gv, gi = lax.top_k(sel, TOP_K)
    expert_idx = g_sel[:, None] * EXPERTS_PER_GROUP + gi
    gates = gv / gv.sum(-1, keepdims=True)
    n_assign = n_tok * TOP_K
    flat_e = expert_idx.reshape(-1)
    order = jnp.argsort(flat_e)
    sorted_e = flat_e[order]
    tok = order // TOP_K
    counts = jnp.zeros((N_EXPERTS,), jnp.int32).at[flat_e].add(1)
    padded = (counts + MOE_BLOCK - 1) // MOE_BLOCK * MOE_BLOCK
    pad_end = jnp.cumsum(padded)
    pad_start = pad_end - padded
    start = jnp.cumsum(counts) - counts
    dest = pad_start[sorted_e] + jnp.arange(n_assign) - start[sorted_e]
    n_blocks = (n_assign + MOE_BLOCK - 1) // MOE_BLOCK + N_EXPERTS
    xbuf = jnp.zeros((n_blocks * MOE_BLOCK, d), x.dtype).at[dest].set(xt[tok])
    blk_e = jnp.clip(jnp.searchsorted(pad_end, jnp.arange(n_blocks) * MOE_BLOCK, side='right'), 0, N_EXPERTS - 1)

    def expert_block(args):
        xb, e = args
        hid = jax.nn.silu(xb @ w_gate[e]) * (xb @ w_up[e])
        return hid @ w_down[e]

    ybuf = lax.map(expert_block, (xbuf.reshape(n_blocks, MOE_BLOCK, d), blk_e)).reshape(-1, d)
    contrib = ybuf[dest] * gates.reshape(-1)[order][:, None].astype(x.dtype)
    y = jnp.zeros((n_tok, d), x.dtype).at[tok].add(contrib)
    return y.reshape(bsz, t_len, d)


def trunk_layer(x, start, prev, lp, router_w, router_b):
    (w_in, w_out, sinks, b_i, b_f, ml_norm_w, ln1_g, ln1_b, ln2_g, ln2_b, w_gate, w_up, w_down) = lp
    bsz, t_len, _ = x.shape
    pos = start + jnp.arange(t_len, dtype=jnp.int32)
    qa, ka, va, qd, kd, vd, qm, km, vm, om, im, fm = split_columns(x @ w_in)
    qa = partial_rope(qa.reshape(bsz, t_len, SWA_HEADS, HEAD_DIM), pos)
    ka = partial_rope(ka.reshape(bsz, t_len, SWA_KV_HEADS, HEAD_DIM), pos)
    va = va.reshape(bsz, t_len, SWA_KV_HEADS, HEAD_DIM)
    qd = partial_rope(qd.reshape(bsz, t_len, DIL_HEADS, HEAD_DIM), pos)
    kd = partial_rope(kd.reshape(bsz, t_len, DIL_HEADS, HEAD_DIM), pos)
    vd = vd.reshape(bsz, t_len, DIL_HEADS, HEAD_DIM)
    qm = qm.reshape(bsz, t_len, MLSTM_HEADS, MLSTM_DK) * (MLSTM_DK ** -0.5)
    km = km.reshape(bsz, t_len, MLSTM_HEADS, MLSTM_DK)
    vm = vm.reshape(bsz, t_len, MLSTM_HEADS, MLSTM_DV)
    ig = im.astype(F32) + b_i.astype(F32)
    lf = jax.nn.log_sigmoid(fm.astype(F32) + b_f.astype(F32))
    if prev is None:
        swa_kc, swa_vc, dil_kc, dil_vc = ka, va, kd, vd
        swa_keep = min(SWA_WINDOW, t_len)
        dil_keep = min(DIL_WINDOW_MAX, t_len)
        c0 = jnp.zeros((bsz, MLSTM_HEADS, MLSTM_DK, MLSTM_DV), F32)
        n0 = jnp.zeros((bsz, MLSTM_HEADS, MLSTM_DK), F32)
        m0 = jnp.zeros((bsz, MLSTM_HEADS), F32)
    else:
        swa_k_pre, swa_v_pre, dil_k_pre, dil_v_pre, c0, n0, m0 = prev
        swa_kc = jnp.concatenate([swa_k_pre.astype(ka.dtype), ka], axis=1)
        swa_vc = jnp.concatenate([swa_v_pre.astype(va.dtype), va], axis=1)
        dil_kc = jnp.concatenate([dil_k_pre.astype(kd.dtype), kd], axis=1)
        dil_vc = jnp.concatenate([dil_v_pre.astype(vd.dtype), vd], axis=1)
        swa_keep = swa_k_pre.shape[1]
        dil_keep = dil_k_pre.shape[1]
    o_a = swa_with_sinks(qa, swa_kc, swa_vc, swa_kc.shape[1] - t_len, sinks)
    o_d = dilated_attention(qd, dil_kc, dil_vc, dil_kc.shape[1] - t_len)
    h, c, n, m = mlstm_chunkwise(qm, km, vm, ig, lf, c0, n0, m0)
    o_m = (head_layer_norm(h, ml_norm_w) * jax.nn.sigmoid(om.astype(F32))).astype(x.dtype)
    mix = jnp.concatenate([o_a, o_d, o_m], axis=-1) @ w_out
    x = layer_norm(DEEPNORM_ALPHA * x + mix, ln1_g, ln1_b)
    x = layer_norm(DEEPNORM_ALPHA * x + moe_ffn(x, router_w, router_b, w_gate, w_up, w_down), ln2_g, ln2_b)
    new_state = (swa_kc[:, swa_kc.shape[1] - swa_keep:], swa_vc[:, swa_vc.shape[1] - swa_keep:],
                 dil_kc[:, dil_kc.shape[1] - dil_keep:], dil_vc[:, dil_vc.shape[1] - dil_keep:], c, n, m)
    return x, new_state


def setup_inputs(seed: int = 0) -> dict:
    key = jax.random.key(seed)
    ks = jax.random.split(key, 26)
    swa_buf = min(SWA_WINDOW, PAST_LEN)
    dil_buf = min(DIL_WINDOW_MAX, PAST_LEN)
    nrm = jax.random.normal
    return {
        'x_prompt': nrm(ks[0], (BATCH, SEQ, D_MODEL), F32),
        'x_sample': nrm(ks[1], (DEC_BATCH, DEC_SEQ, D_MODEL), F32),
        'cache_swa_k': nrm(ks[2], (DEPTH, DEC_BATCH, swa_buf, SWA_KV_HEADS, HEAD_DIM), F32),
        'cache_swa_v': nrm(ks[3], (DEPTH, DEC_BATCH, swa_buf, SWA_KV_HEADS, HEAD_DIM), F32),
        'cache_dil_k': nrm(ks[4], (DEPTH, DEC_BATCH, dil_buf, DIL_HEADS, HEAD_DIM), F32),
        'cache_dil_v': nrm(ks[5], (DEPTH, DEC_BATCH, dil_buf, DIL_HEADS, HEAD_DIM), F32),
        'state_mlstm_c': 0.3 * nrm(ks[6], (DEPTH, DEC_BATCH, MLSTM_HEADS, MLSTM_DK, MLSTM_DV), F32),
        'state_mlstm_n': 0.3 * nrm(ks[7], (DEPTH, DEC_BATCH, MLSTM_HEADS, MLSTM_DK), F32),
        'state_mlstm_m': 0.5 * nrm(ks[8], (DEPTH, DEC_BATCH, MLSTM_HEADS), F32),
        'w_in': nrm(ks[9], (DEPTH, D_MODEL, IN_COLS), F32) * D_MODEL ** -0.5,
        'w_out': nrm(ks[10], (DEPTH, MIX_WIDTH, D_MODEL), F32) * (MIX_WIDTH ** -0.5 * DEEPNORM_BETA),
        'attn_sinks': 0.5 * nrm(ks[11], (DEPTH, SWA_HEADS), F32),
        'mlstm_b_i': 0.1 * nrm(ks[12], (DEPTH, MLSTM_HEADS), F32),
        'mlstm_b_f': 3.0 + 0.1 * nrm(ks[13], (DEPTH, MLSTM_HEADS), F32),
        'mlstm_norm_w': 1.0 + 0.01 * nrm(ks[14], (DEPTH, MLSTM_HEADS * MLSTM_DV), F32),
        'ln1_g': 1.0 + 0.01 * nrm(ks[15], (DEPTH, D_MODEL), F32),
        'ln1_b': 0.01 * nrm(ks[16], (DEPTH, D_MODEL), F32),
        'ln2_g': 1.0 + 0.01 * nrm(ks[17], (DEPTH, D_MODEL), F32),
        'ln2_b': 0.01 * nrm(ks[18], (DEPTH, D_MODEL), F32),
        'router_w': nrm(ks[19], (D_MODEL, N_EXPERTS), F32) * D_MODEL ** -0.5,
        'router_b': 0.01 * nrm(ks[20], (N_EXPERTS,), F32),
        'moe_w_gate': nrm(ks[21], (DEPTH, N_EXPERTS, D_MODEL, D_EXPERT), F32) * D_MODEL ** -0.5,
        'moe_w_up': nrm(ks[22], (DEPTH, N_EXPERTS, D_MODEL, D_EXPERT), F32) * D_MODEL ** -0.5,
        'moe_w_down': nrm(ks[23], (DEPTH, N_EXPERTS, D_EXPERT, D_MODEL), F32) * (D_EXPERT ** -0.5 * DEEPNORM_BETA),
    }


def reference(x_prompt, x_sample, cache_swa_k, cache_swa_v, cache_dil_k, cache_dil_v,
              state_mlstm_c, state_mlstm_n, state_mlstm_m,
              w_in, w_out, attn_sinks, mlstm_b_i, mlstm_b_f, mlstm_norm_w,
              ln1_g, ln1_b, ln2_g, ln2_b, router_w, router_b, moe_w_gate, moe_w_up, moe_w_down):
    xp, xs = x_prompt, x_sample
    p_states, s_states = [], []
    for l in range(DEPTH):
        lp = (w_in[l], w_out[l], attn_sinks[l], mlstm_b_i[l], mlstm_b_f[l], mlstm_norm_w[l],
              ln1_g[l], ln1_b[l], ln2_g[l], ln2_b[l], moe_w_gate[l], moe_w_up[l], moe_w_down[l])
        xp, sp = trunk_layer(xp, 0, None, lp, router_w, router_b)
        prev = (cache_swa_k[l], cache_swa_v[l], cache_dil_k[l], cache_dil_v[l],
                state_mlstm_c[l], state_mlstm_n[l], state_mlstm_m[l])
        xs, ss = trunk_layer(xs, PAST_LEN, prev, lp, router_w, router_b)
        p_states.append(sp)
        s_states.append(ss)
    p_swa_k, p_swa_v, p_dil_k, p_dil_v, p_mlstm_c, p_mlstm_n, p_mlstm_m = [jnp.stack(a) for a in zip(*p_states)]
    s_swa_k, s_swa_v, s_dil_k, s_dil_v, s_mlstm_c, s_mlstm_n, s_mlstm_m = [jnp.stack(a) for a in zip(*s_states)]
    return (xp, xs, p_swa_k, p_swa_v, p_dil_k, p_dil_v, p_mlstm_c, p_mlstm_n, p_mlstm_m,
            s_swa_k, s_swa_v, s_dil_k, s_dil_v, s_mlstm_c, s_mlstm_n, s_mlstm_m)
```

```python
import functools

import jax
import jax.numpy as jnp
from jax import lax
from jax.experimental import pallas as pl
from jax.experimental.pallas import tpu as pltpu

F32 = jnp.float32
BF16 = jnp.bfloat16
I32 = jnp.int32

D_MODEL = 1024
DEPTH = 4
HEAD_DIM = 64
SWA_HEADS = 4
SWA_KV_HEADS = 2
SWA_WINDOW = 128
DIL_HEADS = 4
DIL_PAIRS = ((128, 1), (512, 4), (2048, 16))
DIL_WINDOW_MAX = 2048
MLSTM_HEADS = 4
MLSTM_DK = 128
MLSTM_DV = 128
ROPE_DIM = HEAD_DIM // 4
ROPE_THETA = 500000.0
N_EXPERTS = 16
N_EXPERT_GROUPS = 4
EXPERTS_PER_GROUP = 4
TOP_K = 2
D_EXPERT = 512
LN_EPS = 1e-5
DEEPNORM_ALPHA = (2.0 * DEPTH) ** 0.25

_QA, _KA, _VA, _QD, _KD, _VD, _QM, _KM, _VM, _OM, _GATES = 0, 256, 384, 512, 768, 1024, 1280, 1792, 2304, 2816, 3328
_MAIN_COLS = 3328

LANES = 128
VMEM_LIMIT = 56 * 1024 * 1024


def _cparams(sem):
    return pltpu.CompilerParams(dimension_semantics=sem, vmem_limit_bytes=VMEM_LIMIT)


def _in_proj_kernel(x_ref, w_ref, wg_ref, wgt_ref, cos_ref, sa_ref, sb_ref,
                    qa_ref, ka_ref, va_ref, qd_ref, kd_ref, vd_ref,
                    qm_ref, km_ref, vm_ref, om_ref, gc_ref, gr_ref):
    xb = x_ref[...].astype(BF16)
    cos = cos_ref[...]
    sa = sa_ref[...]
    sb = sb_ref[...]

    def seg(c0, n):
        return jnp.dot(xb, w_ref[:, c0:c0 + n], preferred_element_type=F32)

    def rope_store(out_ref, c0, n):
        for j in range(n // LANES):
            v = seg(c0 + j * LANES, LANES)
            r = v * cos + pltpu.roll(v, LANES - ROPE_DIM // 2, 1) * sa + pltpu.roll(v, ROPE_DIM // 2, 1) * sb
            out_ref[:, j * LANES:(j + 1) * LANES] = r.astype(out_ref.dtype)

    rope_store(qa_ref, _QA, 256)
    rope_store(ka_ref, _KA, 128)
    va_ref[...] = seg(_VA, 128)
    rope_store(qd_ref, _QD, 256)
    rope_store(kd_ref, _KD, 256)
    vd_ref[...] = seg(_VD, 256)
    qm_ref[...] = (seg(_QM, 512) * (MLSTM_DK ** -0.5)).astype(BF16)
    km_ref[...] = seg(_KM, 512).astype(BF16)
    vm_ref[...] = seg(_VM, 512).astype(BF16)
    om_ref[...] = seg(_OM, 512)
    gc_ref[...] = jnp.dot(xb, wg_ref[...], preferred_element_type=F32)
    gr_ref[...] = lax.dot_general(wgt_ref[...], xb, (((1,), (1,)), ((), ())), preferred_element_type=F32)


def in_proj(x, w_main, w_g, w_gt, cos, sa, sb, tm=512):
    nt = x.shape[0]
    row = lambda n: pl.BlockSpec((tm, n), lambda i: (i, 0))
    full = lambda a: pl.BlockSpec(a.shape, lambda i: (0, 0))
    outs = [(256, BF16), (128, F32), (128, F32), (256, F32), (256, F32), (256, F32),
            (512, BF16), (512, BF16), (512, BF16), (512, F32), (8, F32)]
    out_shape = [jax.ShapeDtypeStruct((nt, n), dt) for n, dt in outs] + [jax.ShapeDtypeStruct((8, nt), F32)]
    out_specs = [row(n) for n, _ in outs] + [pl.BlockSpec((8, tm), lambda i: (0, i))]
    return pl.pallas_call(
        _in_proj_kernel,
        grid=(nt // tm,),
        in_specs=[row(D_MODEL), full(w_main), full(w_g), full(w_gt), row(LANES), row(LANES), row(LANES)],
        out_specs=out_specs,
        out_shape=out_shape,
        compiler_params=_cparams(("parallel",)),
        name="in_proj",
    )(x, w_main, w_g, w_gt, cos, sa, sb)


def rope_tables(pos):
    half = ROPE_DIM // 2
    inv_freq = ROPE_THETA ** (-jnp.arange(half, dtype=F32) / half)
    ang = pos.astype(F32)[:, None] * inv_freq[None, :]
    c, s = jnp.cos(ang), jnp.sin(ang)
    n = pos.shape[0]
    one = jnp.ones((n, HEAD_DIM - ROPE_DIM), F32)
    zero = jnp.zeros((n, HEAD_DIM - ROPE_DIM), F32)
    zh = jnp.zeros((n, half), F32)
    cos64 = jnp.concatenate([c, c, one], axis=1)
    sa64 = jnp.concatenate([-s, zh, zero], axis=1)
    sb64 = jnp.concatenate([zh, s, zero], axis=1)
    tile2 = lambda a: jnp.concatenate([a, a], axis=1)
    return tile2(cos64), tile2(sa64), tile2(sb64)


def _swa_prompt_kernel(sink_ref, q_ref, kp_ref, kc_ref, vp_ref, vc_ref, o_ref):
    i = pl.program_id(1)
    bq = q_ref.shape[0]
    r = lax.broadcasted_iota(I32, (bq, bq), 0)
    c = lax.broadcasted_iota(I32, (bq, bq), 1)
    mask_prev = (c >= r) & (i > 0)
    mask_cur = c <= r
    scale = HEAD_DIM ** -0.5
    grp = SWA_HEADS // SWA_KV_HEADS
    nt = (((1,), (1,)), ((), ()))
    for g in range(SWA_KV_HEADS):
        sl = slice(g * HEAD_DIM, (g + 1) * HEAD_DIM)
        kp = kp_ref[:, sl].astype(BF16)
        kc = kc_ref[:, sl].astype(BF16)
        vp = vp_ref[:, sl].astype(BF16)
        vc = vc_ref[:, sl].astype(BF16)
        for h in range(g * grp, (g + 1) * grp):
            qh = q_ref[:, h * HEAD_DIM:(h + 1) * HEAD_DIM]
            sp = jnp.where(mask_prev, lax.dot_general(qh, kp, nt, preferred_element_type=F32) * scale, -jnp.inf)
            sc = jnp.where(mask_cur, lax.dot_general(qh, kc, nt, preferred_element_type=F32) * scale, -jnp.inf)
            sink = sink_ref[h]
            m = jnp.maximum(jnp.maximum(sp.max(-1, keepdims=True), sc.max(-1, keepdims=True)), sink)
            pp = jnp.exp(sp - m)
            pc = jnp.exp(sc - m)
            den = pp.sum(-1, keepdims=True) + pc.sum(-1, keepdims=True) + jnp.exp(sink - m)
            o = (jnp.dot(pp.astype(BF16), vp, preferred_element_type=F32)
                 + jnp.dot(pc.astype(BF16), vc, preferred_element_type=F32)) / den
            o_ref[:, h * HEAD_DIM:(h + 1) * HEAD_DIM] = o.astype(o_ref.dtype)


def swa_prompt(sinks, qa, ka, va, bsz, t_len, bq=128):
    nb = t_len // bq
    cur = lambda b, i: (b * nb + i, 0)
    prev = lambda b, i: (b * nb + jnp.maximum(i - 1, 0), 0)
    kv = SWA_KV_HEADS * HEAD_DIM
    return pl.pallas_call(
        _swa_prompt_kernel,
        grid=(bsz, nb),
        in_specs=[pl.BlockSpec(memory_space=pltpu.SMEM),
                  pl.BlockSpec((bq, SWA_HEADS * HEAD_DIM), cur),
                  pl.BlockSpec((bq, kv), prev), pl.BlockSpec((bq, kv), cur),
                  pl.BlockSpec((bq, kv), prev), pl.BlockSpec((bq, kv), cur)],
        out_specs=pl.BlockSpec((bq, SWA_HEADS * HEAD_DIM), cur),
        out_shape=jax.ShapeDtypeStruct((bsz * t_len, SWA_HEADS * HEAD_DIM), BF16),
        compiler_params=_cparams(("parallel", "parallel")),
        name="swa_prompt",
    )(sinks, qa, ka, ka, va, va)


DIL_TILE = 2048
DIL_BLOCK = 128


def _dil_unit(qv, kp, kc, vp, vc, prev_valid):
    n = DIL_BLOCK
    r = lax.broadcasted_iota(I32, (n, n), 0)
    c = lax.broadcasted_iota(I32, (n, n), 1)
    mask_prev = (c >= r) & prev_valid
    mask_cur = c <= r
    scale = HEAD_DIM ** -0.5
    nt = (((1,), (1,)), ((), ()))
    accs, ms, ls = [], [], []
    for h in range(LANES // HEAD_DIM):
        sl = slice(h * HEAD_DIM, (h + 1) * HEAD_DIM)
        qh = qv[:, sl].astype(BF16)
        sp = jnp.where(mask_prev, lax.dot_general(qh, kp[:, sl].astype(BF16), nt, preferred_element_type=F32) * scale, -jnp.inf)
        sc = jnp.where(mask_cur, lax.dot_general(qh, kc[:, sl].astype(BF16), nt, preferred_element_type=F32) * scale, -jnp.inf)
        m = jnp.maximum(sp.max(-1, keepdims=True), sc.max(-1, keepdims=True))
        pp = jnp.exp(sp - m)
        pc = jnp.exp(sc - m)
        l = pp.sum(-1, keepdims=True) + pc.sum(-1, keepdims=True)
        acc = (jnp.dot(pp.astype(BF16), vp[:, sl].astype(BF16), preferred_element_type=F32)
               + jnp.dot(pc.astype(BF16), vc[:, sl].astype(BF16), preferred_element_type=F32))
        accs.append(acc)
        ms.append(jnp.broadcast_to(m, (n, HEAD_DIM)))
        ls.append(jnp.broadcast_to(l, (n, HEAD_DIM)))
    cat = lambda xs: jnp.concatenate(xs, axis=1)
    return cat(accs), cat(ms), cat(ls)


def _dil_prompt_kernel(q_ref, kp_ref, kc_ref, vp_ref, vc_ref, o_ref, a_ref, m_ref, l_ref):
    has_prev_tile = pl.program_id(1) > 0
    n = DIL_BLOCK

    def merge(rows, res, first):
        acc, m, l = res
        if first:
            a_ref[rows, :] = acc
            m_ref[rows, :] = m
            l_ref[rows, :] = l
        else:
            mo = m_ref[rows, :]
            mn = jnp.maximum(mo, m)
            wa = jnp.exp(mo - mn)
            wb = jnp.exp(m - mn)
            a_ref[rows, :] = a_ref[rows, :] * wa + acc * wb
            l_ref[rows, :] = l_ref[rows, :] * wa + l * wb
            m_ref[rows, :] = mn

    for d in (1, 4, 16):
        first = d == 1
        span = n * d
        for res in range(d):
            rows0 = pl.ds(res, n, stride=d) if d > 1 else pl.ds(0, n)
            prow = pl.ds(DIL_TILE - span + res, n, stride=d) if d > 1 else pl.ds(DIL_TILE - n, n)
            out = _dil_unit(q_ref[rows0, :], kp_ref[prow, :], kc_ref[rows0, :], vp_ref[prow, :], vc_ref[rows0, :],
                            has_prev_tile)
            merge(rows0, out, first)

            if DIL_TILE // span > 1:
                def body(j, carry, d=d, res=res, span=span, first=first):
                    base = pl.multiple_of((j - 1) * span, span)
                    win = pl.ds(base, 2 * span)
                    if d > 1:
                        prow_j = pl.ds(res, n, stride=d)
                        crow_j = pl.ds(span + res, n, stride=d)
                    else:
                        prow_j = pl.ds(0, n)
                        crow_j = pl.ds(n, n)
                    qv = q_ref.at[win, :][crow_j, :]
                    out_j = _dil_unit(qv, kc_ref.at[win, :][prow_j, :], kc_ref.at[win, :][crow_j, :],
                                      vc_ref.at[win, :][prow_j, :], vc_ref.at[win, :][crow_j, :], True)
                    acc, m, l = out_j
                    if first:
                        a_ref.at[win, :][crow_j, :] = acc
                        m_ref.at[win, :][crow_j, :] = m
                        l_ref.at[win, :][crow_j, :] = l
                    else:
                        av, mv, lv = a_ref.at[win, :], m_ref.at[win, :], l_ref.at[win, :]
                        mo = mv[crow_j, :]
                        mn = jnp.maximum(mo, m)
                        wa = jnp.exp(mo - mn)
                        wb = jnp.exp(m - mn)
                        av[crow_j, :] = av[crow_j, :] * wa + acc * wb
                        lv[crow_j, :] = lv[crow_j, :] * wa + l * wb
                        mv[crow_j, :] = mn
                    return carry
                lax.fori_loop(1, DIL_TILE // span, body, 0)

    o_ref[...] = (a_ref[...] / l_ref[...]).astype(o_ref.dtype)


def dil_prompt(qd, kd, vd, bsz, t_len):
    ntile = t_len // DIL_TILE
    cur = lambda b, i, p: (b * ntile + i, p)
    prev = lambda b, i, p: (b * ntile + jnp.maximum(i - 1, 0), p)
    w = DIL_HEADS * HEAD_DIM
    blk = lambda im: pl.BlockSpec((DIL_TILE, LANES), im)
    return pl.pallas_call(
        _dil_prompt_kernel,
        grid=(bsz, ntile, w // LANES),
        in_specs=[blk(cur), blk(prev), blk(cur), blk(prev), blk(cur)],
        out_specs=blk(cur),
        out_shape=jax.ShapeDtypeStruct((bsz * t_len, w), BF16),
        scratch_shapes=[pltpu.VMEM((DIL_TILE, LANES), F32)] * 3,
        compiler_params=_cparams(("parallel", "parallel", "parallel")),
        name="dil_prompt",
    )(qd, kd, kd, vd, vd)


MLSTM_CHUNK = 256
_HI = lax.Precision.HIGHEST


def _log_sigmoid(x):
    return jnp.minimum(x, 0.0) - jnp.log1p(jnp.exp(-jnp.abs(x)))


def _mlstm_prompt_kernel(q_ref, k_ref, v_ref, om_ref, gc_ref, gr_ref, brow_ref, bcol_ref, nw_ref,
                         o_ref, c_ref, n_ref, m_ref):
    L = q_ref.shape[0]

    @pl.when(pl.program_id(1) == 0)
    def _():
        c_ref[...] = jnp.zeros_like(c_ref)
        n_ref[...] = jnp.zeros_like(n_ref)
        m_ref[...] = jnp.zeros_like(m_ref)

    r = lax.broadcasted_iota(I32, (L, L), 0)
    c = lax.broadcasted_iota(I32, (L, L), 1)
    causal = c <= r
    tril = causal.astype(F32)
    triu = (r <= c).astype(F32)
    g_col = gc_ref[...] + brow_ref[...]
    g_row = gr_ref[...] + bcol_ref[...]
    bcum_col = jnp.dot(tril, _log_sigmoid(g_col), precision=_HI, preferred_element_type=F32)
    bcum_row = jnp.dot(_log_sigmoid(g_row), triu, precision=_HI, preferred_element_type=F32)
    nt = (((1,), (1,)), ((), ()))
    H = MLSTM_HEADS
    for h in range(H):
        sl = slice(h * MLSTM_DK, (h + 1) * MLSTM_DK)
        ig_col, ig_row = g_col[:, h:h + 1], g_row[h:h + 1, :]
        b_col, b_row = bcum_col[:, H + h:H + h + 1], bcum_row[H + h:H + h + 1, :]
        m_prev = m_ref[0, h, :, 0:1]
        c_prev = c_ref[0, h]
        n_prev = n_ref[0, h]
        q, k, v = q_ref[:, sl], k_ref[:, sl], v_ref[:, sl]

        log_d = jnp.where(causal, b_col - b_row + ig_row, -jnp.inf)
        log_inter = b_col + m_prev
        m_t = jnp.maximum(log_inter, log_d.max(-1, keepdims=True))
        w_ts = jnp.exp(log_d - m_t) * lax.dot_general(q, k, nt, preferred_element_type=F32)
        w_inter = jnp.exp(log_inter - m_t)
        num = (jnp.dot(w_ts.astype(BF16), v, preferred_element_type=F32)
               + w_inter * jnp.dot(q, c_prev.astype(BF16), preferred_element_type=F32))
        den = w_ts.sum(-1, keepdims=True) + w_inter * (q.astype(F32) * n_prev).sum(-1, keepdims=True)
        hh = num / jnp.maximum(jnp.abs(den), jnp.exp(-m_t))

        mu = hh.mean(-1, keepdims=True)
        var = jnp.mean(jnp.square(hh - mu), -1, keepdims=True)
        hn = (hh - mu) * lax.rsqrt(var + LN_EPS) * nw_ref[:, sl]
        o_ref[:, sl] = (hn * jax.nn.sigmoid(om_ref[:, sl])).astype(o_ref.dtype)

        b_last = b_col[L - 1:L, :]
        m_new = jnp.maximum(b_last + m_prev, (b_last - b_row + ig_row).max(-1, keepdims=True))
        w_s = jnp.exp(b_last - b_col + ig_col - m_new)
        decay = jnp.exp(b_last + m_prev - m_new)
        kw = w_s * k.astype(F32)
        c_ref[0, h] = decay * c_prev + jnp.dot(kw.T.astype(BF16), v, preferred_element_type=F32)
        n_ref[0, h] = decay * n_prev + kw.sum(0, keepdims=True)
        m_ref[0, h] = jnp.broadcast_to(m_new, (1, LANES))


def mlstm_prompt(qm, km, vm, om, gc, gr, bias_row, bias_col, norm_w, bsz, t_len):
    L = MLSTM_CHUNK
    nc = t_len // L
    H = MLSTM_HEADS
    w = H * MLSTM_DV
    row = lambda n: pl.BlockSpec((L, n), lambda b, j: (b * nc + j, 0))
    const = lambda shape: pl.BlockSpec(shape, lambda b, j: (0,) * len(shape))
    st = lambda *tail: pl.BlockSpec((1, H) + tail, lambda b, j: (b, 0) + (0,) * len(tail))
    return pl.pallas_call(
        _mlstm_prompt_kernel,
        grid=(bsz, nc),
        in_specs=[row(w), row(w), row(w), row(w), row(8), pl.BlockSpec((8, L), lambda b, j: (0, b * nc + j)),
                  const((1, 8)), const((8, 1)), const((1, w))],
        out_specs=[row(w), st(MLSTM_DK, MLSTM_DV), st(1, MLSTM_DK), st(1, LANES)],
        out_shape=[jax.ShapeDtypeStruct((bsz * t_len, w), BF16),
                   jax.ShapeDtypeStruct((bsz, H, MLSTM_DK, MLSTM_DV), F32),
                   jax.ShapeDtypeStruct((bsz, H, 1, MLSTM_DK), F32),
                   jax.ShapeDtypeStruct((bsz, H, 1, LANES), F32)],
        compiler_params=_cparams(("parallel", "arbitrary")),
        name="mlstm_prompt",
    )(qm, km, vm, om, gc, gr, bias_row, bias_col, norm_w)


def _layer_norm(y, g, b):
    mu = y.mean(-1, keepdims=True)
    var = jnp.mean(jnp.square(y - mu), -1, keepdims=True)
    return (y - mu) * lax.rsqrt(var + LN_EPS) * g + b


def _group_partner(a, j, sub):
    up = pltpu.roll(a, N_EXPERTS - j, 0)
    down = pltpu.roll(a, EXPERTS_PER_GROUP - j, 0)
    return jnp.where(sub + j < EXPERTS_PER_GROUP, up, down)


def _route(logits_t):
    n = logits_t.shape[1]
    e_id = lax.broadcasted_iota(I32, (N_EXPERTS, n), 0)
    sub = e_id % EXPERTS_PER_GROUP
    grp = e_id // EXPERTS_PER_GROUP
    ex = jnp.exp(logits_t - logits_t.max(0, keepdims=True))
    aff = ex / ex.sum(0, keepdims=True)
    rank = jnp.zeros((N_EXPERTS, n), I32)
    for j in range(1, EXPERTS_PER_GROUP):
        other = _group_partner(aff, j, sub)
        other_sub = (sub + j) % EXPERTS_PER_GROUP
        rank += ((other > aff) | ((other == aff) & (other_sub < sub))).astype(I32)
    top = jnp.where(rank < TOP_K, aff, 0.0)
    score = top
    for j in range(1, EXPERTS_PER_GROUP):
        score = score + _group_partner(top, j, sub)
    grank = jnp.zeros((N_EXPERTS, n), I32)
    for j in range(1, N_EXPERT_GROUPS):
        other = pltpu.roll(score, N_EXPERTS - EXPERTS_PER_GROUP * j, 0)
        other_grp = (grp + j) % N_EXPERT_GROUPS
        grank += ((other > score) | ((other == score) & (other_grp < grp))).astype(I32)
    chosen = grank == 0
    ids, vals = [], []
    for slot in range(TOP_K):
        pick = chosen & (rank == slot)
        ids.append(jnp.where(pick, e_id, 0).sum(0, keepdims=True))
        vals.append(jnp.where(pick, aff, 0.0).sum(0, keepdims=True))
    total = vals[0] + vals[1]
    return ids, [v / total for v in vals]


def _out_proj_kernel(n_prompt_tiles, oap_ref, odp_ref, omp_ref, oas_ref, ods_ref, oms_ref, x_ref, w_ref,
                     g_ref, b_ref, rwt_ref, rb_ref, x1_ref, e_ref, gate_ref):
    is_prompt = pl.program_id(0) < n_prompt_tiles
    pick = lambda p_ref, s_ref: jnp.where(is_prompt, p_ref[...], s_ref[...])
    o_a, o_d, o_m = pick(oap_ref, oas_ref), pick(odp_ref, ods_ref), pick(omp_ref, oms_ref)
    na, nd = o_a.shape[1], o_d.shape[1]
    mix = (jnp.dot(o_a, w_ref[0:na, :], preferred_element_type=F32)
           + jnp.dot(o_d, w_ref[na:na + nd, :], preferred_element_type=F32)
           + jnp.dot(o_m, w_ref[na + nd:, :], preferred_element_type=F32))
    x1 = _layer_norm(DEEPNORM_ALPHA * x_ref[...] + mix, g_ref[...], b_ref[...])
    x1_ref[...] = x1
    logits_t = lax.dot_general(rwt_ref[...], x1, (((1,), (1,)), ((), ())), precision=_HI,
                               preferred_element_type=F32) + rb_ref[...]
    ids, gates = _route(logits_t)
    for slot in range(TOP_K):
        e_ref[slot:slot + 1, :] = ids[slot]
        gate_ref[slot:slot + 1, :] = gates[slot]


def out_proj(oa_p, od_p, om_p, oa_s, od_s, om_s, x, w_out, ln_g, ln_b, rw_t, rb_col, tm=512):
    nt = x.shape[0]
    n_p = oa_p.shape[0]
    assert n_p % tm == 0 and oa_s.shape[0] == tm and nt == n_p + tm
    npt = n_p // tm
    prow = lambda n: pl.BlockSpec((tm, n), lambda i: (jnp.minimum(i, npt - 1), 0))
    srow = lambda n: pl.BlockSpec((tm, n), lambda i: (0, 0))
    row = lambda n: pl.BlockSpec((tm, n), lambda i: (i, 0))
    full = lambda a: pl.BlockSpec(a.shape, lambda i: (0, 0))
    col = pl.BlockSpec((TOP_K, tm), lambda i: (0, i))
    return pl.pallas_call(
        functools.partial(_out_proj_kernel, npt),
        grid=(nt // tm,),
        in_specs=[prow(256), prow(256), prow(512), srow(256), srow(256), srow(512), row(D_MODEL), full(w_out),
                  full(ln_g), full(ln_b), full(rw_t), full(rb_col)],
        out_specs=[row(D_MODEL), col, col],
        out_shape=[jax.ShapeDtypeStruct((nt, D_MODEL), F32), jax.ShapeDtypeStruct((TOP_K, nt), I32),
                   jax.ShapeDtypeStruct((TOP_K, nt), F32)],
        compiler_params=_cparams(("parallel",)),
        name="out_proj",
    )(oa_p, od_p, om_p, oa_s, od_s, om_s, x, w_out, ln_g, ln_b, rw_t, rb_col)


MOE_ROWS = 256


def moe_plan(e_ids, gates, n_tok):
    n_assign = TOP_K * n_tok
    n_blocks = -(-n_assign // MOE_ROWS) + N_EXPERTS
    flat_e = e_ids.reshape(-1)
    onehot = (flat_e[:, None] == jnp.arange(N_EXPERTS, dtype=I32)[None, :]).astype(I32)
    csum = jnp.cumsum(onehot, axis=0)
    rank = jnp.take_along_axis(csum, flat_e[:, None], axis=1)[:, 0] - 1
    counts = csum[-1]
    padded = (counts + MOE_ROWS - 1) // MOE_ROWS * MOE_ROWS
    pad_end = jnp.cumsum(padded)
    dest = (pad_end - padded)[flat_e] + rank
    blk_e = jnp.clip(jnp.searchsorted(pad_end, jnp.arange(n_blocks, dtype=I32) * MOE_ROWS, side='right'),
                     0, N_EXPERTS - 1).astype(I32)
    n_used = (pad_end[-1] // MOE_ROWS).astype(I32).reshape(1)
    n_rows = n_blocks * MOE_ROWS
    spare = n_assign + jnp.arange(n_rows, dtype=I32) % MOE_ROWS
    row_dst = spare.at[dest].set(jnp.arange(n_assign, dtype=I32))
    row_gate = jnp.zeros((n_rows,), F32).at[dest].set(gates.reshape(-1))
    return blk_e, n_used, row_dst, row_gate[:, None]


def _moe_kernel(n_tok, blk_e_ref, n_used_ref, row_dst_ref, x_hbm, gate_ref, wg_ref, wu_ref, wd_ref, y_hbm,
                xbuf, ybuf, sem_g, sem_s):
    i = pl.program_id(0)
    R = MOE_ROWS

    @pl.when(i == 0)
    def _():
        ybuf[...] = jnp.zeros_like(ybuf)
        spare = pltpu.make_async_copy(ybuf, y_hbm.at[pl.ds(TOP_K * n_tok, R), :], sem_s)
        spare.start()
        spare.wait()

    @pl.when(i < n_used_ref[0])
    def _():
        base = i * R

        def gather(r, carry):
            d = row_dst_ref[base + r]
            tok = jnp.where(d >= TOP_K * n_tok, 0, jnp.where(d >= n_tok, d - n_tok, d))
            pltpu.make_async_copy(x_hbm.at[pl.ds(tok, 1), :], xbuf.at[pl.ds(r, 1), :], sem_g).start()
            return carry
        lax.fori_loop(0, R, gather, 0)
        pltpu.make_async_copy(x_hbm.at[pl.ds(0, R), :], xbuf, sem_g).wait()

        xb = xbuf[...].astype(BF16)
        hid = (jax.nn.silu(jnp.dot(xb, wg_ref[0], preferred_element_type=F32))
               * jnp.dot(xb, wu_ref[0], preferred_element_type=F32))
        ybuf[...] = jnp.dot(hid.astype(BF16), wd_ref[0], preferred_element_type=F32) * gate_ref[...]

        def scatter(r, carry):
            d = row_dst_ref[base + r]
            pltpu.make_async_copy(ybuf.at[pl.ds(r, 1), :], y_hbm.at[pl.ds(d, 1), :], sem_s).start()
            return carry
        lax.fori_loop(0, R, scatter, 0)
        pltpu.make_async_copy(ybuf, y_hbm.at[pl.ds(0, R), :], sem_s).wait()


def moe_experts(x1, blk_e, n_used, row_dst, row_gate, w_gate, w_up, w_down):
    n_tok = x1.shape[0]
    n_blocks = blk_e.shape[0]
    R = MOE_ROWS
    wspec = lambda shape: pl.BlockSpec((1,) + shape, lambda i, be, nu, rd: (be[i], 0, 0))
    grid_spec = pltpu.PrefetchScalarGridSpec(
        num_scalar_prefetch=3,
        grid=(n_blocks,),
        in_specs=[pl.BlockSpec(memory_space=pl.ANY),
                  pl.BlockSpec((R, 1), lambda i, be, nu, rd: (i, 0)),
                  wspec((D_MODEL, D_EXPERT)), wspec((D_MODEL, D_EXPERT)), wspec((D_EXPERT, D_MODEL))],
        out_specs=pl.BlockSpec(memory_space=pl.ANY),
        scratch_shapes=[pltpu.VMEM((R, D_MODEL), F32), pltpu.VMEM((R, D_MODEL), F32),
                        pltpu.SemaphoreType.DMA(()), pltpu.SemaphoreType.DMA(())],
    )
    return pl.pallas_call(
        functools.partial(_moe_kernel, n_tok),
        grid_spec=grid_spec,
        out_shape=jax.ShapeDtypeStruct((TOP_K * n_tok + R, D_MODEL), F32),
        compiler_params=_cparams(("arbitrary",)),
        name="moe_experts",
    )(blk_e, n_used, row_dst, x1, row_gate, w_gate, w_up, w_down)


def _ln2_kernel(x1_ref, y0_ref, y1_ref, g_ref, b_ref, o_ref):
    o_ref[...] = _layer_norm(DEEPNORM_ALPHA * x1_ref[...] + (y0_ref[...] + y1_ref[...]), g_ref[...], b_ref[...])


def moe_combine_ln2(x1, y, ln_g, ln_b, tm=512):
    nt = x1.shape[0]
    nb = nt // tm
    row = lambda off: pl.BlockSpec((tm, D_MODEL), lambda i: (i + off, 0))
    full = lambda a: pl.BlockSpec(a.shape, lambda i: (0, 0))
    return pl.pallas_call(
        _ln2_kernel,
        grid=(nb,),
        in_specs=[row(0), row(0), row(nb), full(ln_g), full(ln_b)],
        out_specs=row(0),
        out_shape=jax.ShapeDtypeStruct((nt, D_MODEL), F32),
        compiler_params=_cparams(("parallel",)),
        name="moe_combine_ln2",
    )(x1, y, y, ln_g, ln_b)


def _dil_counts(n_new, n_buf, col0, n_col):
    i = lax.broadcasted_iota(I32, (n_new, n_col), 0)
    c = lax.broadcasted_iota(I32, (n_new, n_col), 1) + col0
    delta = n_buf + i - c
    cnt = jnp.zeros((n_new, n_col), F32)
    for window, dil in DIL_PAIRS:
        cnt += ((delta >= 0) & (delta <= window) & (delta % dil == 0)).astype(F32)
    return cnt


def _dil_sample_kernel(q_ref, kn_ref, vn_ref, kt_ref, vt_ref, o_ref):
    bt, n_new = q_ref.shape[0], q_ref.shape[1]
    n_buf = kt_ref.shape[-1]
    cnt_c = _dil_counts(n_new, n_buf, 0, n_buf)
    cnt_n = _dil_counts(n_new, n_buf, n_buf, n_new)
    scale = HEAD_DIM ** -0.5
    nt = (((1,), (1,)), ((), ()))

    def body(b, carry):
        q, kn, vn = q_ref[b], kn_ref[b], vn_ref[b]
        for h in range(DIL_HEADS):
            sl = slice(h * HEAD_DIM, (h + 1) * HEAD_DIM)
            qh = q[:, sl].astype(BF16)
            kt = kt_ref[0, b, h].astype(BF16)
            vt = vt_ref[0, b, h].astype(BF16)
            s_c = jnp.dot(qh, kt, preferred_element_type=F32) * scale
            s_n = lax.dot_general(qh, kn[:, sl].astype(BF16), nt, preferred_element_type=F32) * scale
            m = jnp.maximum(jnp.where(cnt_c > 0, s_c, -jnp.inf).max(-1, keepdims=True),
                            jnp.where(cnt_n > 0, s_n, -jnp.inf).max(-1, keepdims=True))
            e_c = cnt_c * jnp.exp(jnp.where(cnt_c > 0, s_c - m, -jnp.inf))
            e_n = cnt_n * jnp.exp(jnp.where(cnt_n > 0, s_n - m, -jnp.inf))
            den = e_c.sum(-1, keepdims=True) + e_n.sum(-1, keepdims=True)
            o = (lax.dot_general(e_c.astype(BF16), vt, nt, preferred_element_type=F32)
                 + jnp.dot(e_n.astype(BF16), vn[:, sl].astype(BF16), preferred_element_type=F32)) / den
            o_ref[b, :, sl] = o
        return carry
    lax.fori_loop(0, bt, body, 0)


def dil_sample(layer, q, kn, vn, cache_kt, cache_vt, bt=2):
    bsz, n_new, w = q.shape
    n_buf = cache_kt.shape[-1]
    tok = pl.BlockSpec((bt, n_new, w), lambda i: (i, 0, 0))
    buf = pl.BlockSpec((1, bt, DIL_HEADS, HEAD_DIM, n_buf), lambda i: (layer, i, 0, 0, 0))
    return pl.pallas_call(
        _dil_sample_kernel,
        grid=(bsz // bt,),
        in_specs=[tok, tok, tok, buf, buf],
        out_specs=tok,
        out_shape=jax.ShapeDtypeStruct((bsz, n_new, w), F32),
        compiler_params=_cparams(("parallel",)),
        name="dil_sample",
    )(q, kn, vn, cache_kt, cache_vt)


def _swa_sample_kernel(sink_ref, q_ref, kn_ref, vn_ref, kt_ref, vt_ref, o_ref):
    bt, n_new = q_ref.shape[0], q_ref.shape[1]
    n_buf = kt_ref.shape[-1]
    i_c = lax.broadcasted_iota(I32, (n_new, n_buf), 0)
    c_c = lax.broadcasted_iota(I32, (n_new, n_buf), 1)
    mask_c = (c_c >= i_c) & (n_buf + i_c - c_c <= SWA_WINDOW)
    i_n = lax.broadcasted_iota(I32, (n_new, n_new), 0)
    t_n = lax.broadcasted_iota(I32, (n_new, n_new), 1)
    mask_n = t_n <= i_n
    scale = HEAD_DIM ** -0.5
    nt = (((1,), (1,)), ((), ()))
    grp = SWA_HEADS // SWA_KV_HEADS

    def body(b, carry):
        q, kn, vn = q_ref[b], kn_ref[b], vn_ref[b]
        for h in range(SWA_HEADS):
            g = h // grp
            sl = slice(h * HEAD_DIM, (h + 1) * HEAD_DIM)
            gl = slice(g * HEAD_DIM, (g + 1) * HEAD_DIM)
            qh = q[:, sl].astype(BF16)
            kt = kt_ref[0, b, g].astype(BF16)
            vt = vt_ref[0, b, g].astype(BF16)
            s_c = jnp.where(mask_c, jnp.dot(qh, kt, preferred_element_type=F32) * scale, -jnp.inf)
            s_n = jnp.where(mask_n, lax.dot_general(qh, kn[:, gl].astype(BF16), nt, preferred_element_type=F32) * scale,
                            -jnp.inf)
            sink = sink_ref[h]
            m = jnp.maximum(jnp.maximum(s_c.max(-1, keepdims=True), s_n.max(-1, keepdims=True)), sink)
            e_c = jnp.exp(s_c - m)
            e_n = jnp.exp(s_n - m)
            den = e_c.sum(-1, keepdims=True) + e_n.sum(-1, keepdims=True) + jnp.exp(sink - m)
            o = (lax.dot_general(e_c.astype(BF16), vt, nt, preferred_element_type=F32)
                 + jnp.dot(e_n.astype(BF16), vn[:, gl].astype(BF16), preferred_element_type=F32)) / den
            o_ref[b, :, sl] = o
        return carry
    lax.fori_loop(0, bt, body, 0)


def swa_sample(layer, sinks, q, kn, vn, cache_kt, cache_vt, bt=16):
    bsz, n_new, w = q.shape
    n_buf = cache_kt.shape[-1]
    tok = lambda n: pl.BlockSpec((bt, n_new, n), lambda i: (i, 0, 0))
    buf = pl.BlockSpec((1, bt, SWA_KV_HEADS, HEAD_DIM, n_buf), lambda i: (layer, i, 0, 0, 0))
    kv = SWA_KV_HEADS * HEAD_DIM
    return pl.pallas_call(
        _swa_sample_kernel,
        grid=(bsz // bt,),
        in_specs=[pl.BlockSpec(memory_space=pltpu.SMEM), tok(w), tok(kv), tok(kv), buf, buf],
        out_specs=tok(w),
        out_shape=jax.ShapeDtypeStruct((bsz, n_new, w), F32),
        compiler_params=_cparams(("parallel",)),
        name="swa_sample",
    )(sinks, q, kn, vn, cache_kt, cache_vt)


def _mlstm_sample_kernel(q_ref, k_ref, v_ref, qc_ref, kc_ref, om_ref, gc_ref, gr_ref, brow_ref, bcol_ref, nw_ref,
                         c_in, n_in, m_in, o_ref, c_out, n_out, m_out):
    bt, L = q_ref.shape[0], q_ref.shape[1]
    H = MLSTM_HEADS
    r = lax.broadcasted_iota(I32, (L, L), 0)
    c = lax.broadcasted_iota(I32, (L, L), 1)
    causal = c <= r

    def body(b, carry):
        g_col = gc_ref[b] + brow_ref[...]
        g_row = gr_ref[b] + bcol_ref[...]
        for h in range(H):
            sl = slice(h * MLSTM_DK, (h + 1) * MLSTM_DK)
            q, k, v = q_ref[b, :, sl], k_ref[b, :, sl], v_ref[b, :, sl]
            q_col, k_col = qc_ref[b, h], kc_ref[b, h]
            c_prev, n_prev, m_prev = c_in[0, b, h], n_in[0, b, h], m_in[0, b, h][:, 0:1]
            ig_col, ig_row = g_col[:, h:h + 1], g_row[h:h + 1, :]
            lf_col, lf_row = _log_sigmoid(g_col[:, H + h:H + h + 1]), _log_sigmoid(g_row[H + h:H + h + 1, :])
            b_col = jnp.where(causal, lf_row, 0.0).sum(-1, keepdims=True)
            b_row = jnp.where(r <= c, lf_col, 0.0).sum(0, keepdims=True)

            log_d = jnp.where(causal, b_col - b_row + ig_row, -jnp.inf)
            log_inter = b_col + m_prev
            m_t = jnp.maximum(log_inter, log_d.max(-1, keepdims=True))
            s_qk = jnp.zeros((L, L), F32)
            for s in range(L):
                s_qk = jnp.where(c == s, (q * k[s:s + 1, :]).sum(-1, keepdims=True), s_qk)
            w_ts = jnp.exp(log_d - m_t) * s_qk
            w_inter = jnp.exp(log_inter - m_t)
            q_c = jnp.concatenate([(q_col[:, t:t + 1] * c_prev).sum(0, keepdims=True) for t in range(L)], axis=0)
            num = w_inter * q_c
            for s in range(L):
                num = num + w_ts[:, s:s + 1] * v[s:s + 1, :]
            den = w_ts.sum(-1, keepdims=True) + w_inter * (q * n_prev).sum(-1, keepdims=True)
            hh = num / jnp.maximum(jnp.abs(den), jnp.exp(-m_t))

            mu = hh.mean(-1, keepdims=True)
            var = jnp.mean(jnp.square(hh - mu), -1, keepdims=True)
            hn = (hh - mu) * lax.rsqrt(var + LN_EPS) * nw_ref[:, sl]
            o_ref[b, :, sl] = hn * jax.nn.sigmoid(om_ref[b, :, sl])

            b_last = b_col[L - 1:L, :]
            m_new = jnp.maximum(b_last + m_prev, (b_last - b_row + ig_row).max(-1, keepdims=True))
            w_s = jnp.exp(b_last - b_col + ig_col - m_new)
            decay = jnp.exp(b_last + m_prev - m_new)
            wv = w_s * v
            c_new = decay * c_prev
            for s in range(L):
                c_new = c_new + k_col[:, s:s + 1] * wv[s:s + 1, :]
            c_out[b, h] = c_new
            n_out[b, h] = decay * n_prev + (w_s * k).sum(0, keepdims=True)
            m_out[b, h] = jnp.broadcast_to(m_new, (1, LANES))
        return carry
    lax.fori_loop(0, bt, body, 0)


def mlstm_sample(layer, q, k, v, q_col, k_col, om, gc, gr, bias_row, bias_col, norm_w, c_in, n_in, m_in, bt=8):
    bsz, L, w = q.shape
    H = MLSTM_HEADS
    lead = lambda *tail: pl.BlockSpec((bt,) + tail, lambda i: (i,) + (0,) * len(tail))
    const = lambda shape: pl.BlockSpec(shape, lambda i: (0,) * len(shape))
    st_in = lambda *tail: pl.BlockSpec((1, bt, H) + tail, lambda i: (layer, i, 0) + (0,) * len(tail))
    return pl.pallas_call(
        _mlstm_sample_kernel,
        grid=(bsz // bt,),
        in_specs=[lead(L, w), lead(L, w), lead(L, w), lead(H, MLSTM_DK, L), lead(H, MLSTM_DK, L), lead(L, w),
                  lead(L, 8), lead(8, L), const((1, 8)), const((8, 1)), const((1, w)),
                  st_in(MLSTM_DK, MLSTM_DV), st_in(1, MLSTM_DK), st_in(1, LANES)],
        out_specs=[lead(L, w), lead(H, MLSTM_DK, MLSTM_DV), lead(H, 1, MLSTM_DK), lead(H, 1, LANES)],
        out_shape=[jax.ShapeDtypeStruct((bsz, L, w), F32),
                   jax.ShapeDtypeStruct((bsz, H, MLSTM_DK, MLSTM_DV), F32),
                   jax.ShapeDtypeStruct((bsz, H, 1, MLSTM_DK), F32),
                   jax.ShapeDtypeStruct((bsz, H, 1, LANES), F32)],
        compiler_params=_cparams(("parallel",)),
        name="mlstm_sample",
    )(q, k, v, q_col, k_col, om, gc, gr, bias_row, bias_col, norm_w, c_in, n_in, m_in)


def _cache_roll_kernel(k_ref, v_ref, kn_ref, vn_ref, ko_ref, vo_ref):
    shape = k_ref.shape
    n_buf, n_new = shape[-1], kn_ref.shape[-1]
    rows = shape[1] * shape[2] * shape[3]
    lane = lax.broadcasted_iota(I32, (rows, LANES), 1)
    for src, new, dst in ((k_ref, kn_ref, ko_ref), (v_ref, vn_ref, vo_ref)):
        rolled = pltpu.roll(src[0].reshape(rows, n_buf), n_buf - n_new, 1)
        tail = rolled[:, n_buf - LANES:]
        fresh = new[0].reshape(rows, n_new)
        for t in range(n_new):
            tail = jnp.where(lane == LANES - n_new + t, fresh[:, t:t + 1], tail)
        dst[0] = rolled.reshape(shape[1:])
        dst[0, :, :, :, n_buf - LANES:] = tail.reshape(shape[1:4] + (LANES,))


def cache_roll(cache_kt, cache_vt, new_kt, new_vt, bt):
    depth, bsz, h, d, n_buf = cache_kt.shape
    n_new = new_kt.shape[-1]
    blk = lambda n: pl.BlockSpec((1, bt, h, d, n), lambda l, i: (l, i, 0, 0, 0))
    return pl.pallas_call(
        _cache_roll_kernel,
        grid=(depth, bsz // bt),
        in_specs=[blk(n_buf), blk(n_buf), blk(n_new), blk(n_new)],
        out_specs=[blk(n_buf), blk(n_buf)],
        out_shape=[jax.ShapeDtypeStruct(cache_kt.shape, F32)] * 2,
        compiler_params=_cparams(("parallel", "parallel")),
        name="cache_roll",
    )(cache_kt, cache_vt, new_kt, new_vt)


PAST_LEN = 8192


def kernel(x_prompt, x_sample, cache_swa_k, cache_swa_v, cache_dil_k, cache_dil_v, state_mlstm_c, state_mlstm_n, state_mlstm_m, w_in, w_out, attn_sinks, mlstm_b_i, mlstm_b_f, mlstm_norm_w, ln1_g, ln1_b, ln2_g, ln2_b, router_w, router_b, moe_w_gate, moe_w_up, moe_w_down):
    bsz, t_len, d = x_prompt.shape
    sbsz, s_len, _ = x_sample.shape
    n_p, n_s = bsz * t_len, sbsz * s_len
    n_tok = n_p + n_s
    depth = w_in.shape[0]

    x = jnp.concatenate([x_prompt.reshape(n_p, d), x_sample.reshape(n_s, d)], axis=0)
    pos = jnp.concatenate([jnp.tile(jnp.arange(t_len, dtype=I32), bsz),
                           jnp.tile(PAST_LEN + jnp.arange(s_len, dtype=I32), sbsz)])
    cos, sa, sb = rope_tables(pos)

    to_lanes = lambda c: jnp.transpose(c, (0, 1, 3, 4, 2))
    from_lanes = lambda c: jnp.transpose(c, (0, 1, 4, 2, 3))
    swa_kt, swa_vt, dil_kt, dil_vt = map(to_lanes, (cache_swa_k, cache_swa_v, cache_dil_k, cache_dil_v))
    n_in = state_mlstm_n[:, :, :, None, :]
    m_in = jnp.broadcast_to(state_mlstm_m[..., None, None], state_mlstm_m.shape + (1, LANES))
    rw_t, rb_col = router_w.T, router_b[:, None]
    wg_b, wu_b, wd_b = moe_w_gate.astype(BF16), moe_w_up.astype(BF16), moe_w_down.astype(BF16)

    p_states, s_states, new_cols = [], [], []
    for l in range(depth):
        wb = w_in[l].astype(BF16)
        w_g = wb[:, _MAIN_COLS:]
        qa, ka, va, qd, kd, vd, qm, km, vm, om, gc, gr = in_proj(x, wb[:, :_MAIN_COLS], w_g, w_g.T, cos, sa, sb)
        bias = jnp.concatenate([mlstm_b_i[l], mlstm_b_f[l]])
        bias_row, bias_col = bias[None, :], bias[:, None]
        norm_w = mlstm_norm_w[l][None, :]

        oa_p = swa_prompt(attn_sinks[l], qa, ka, va, bsz, t_len)
        od_p = dil_prompt(qd, kd, vd, bsz, t_len)
        om_p, pc, pn, pm = mlstm_prompt(qm, km, vm, om, gc, gr, bias_row, bias_col, norm_w, bsz, t_len)

        s3 = lambda a: a[n_p:].astype(F32).reshape(sbsz, s_len, a.shape[1])
        qa_s, ka_s, va_s, qd_s, kd_s, vd_s, qm_s, km_s, vm_s, om_s, gc_s = map(
            s3, (qa, ka, va, qd, kd, vd, qm, km, vm, om, gc))
        gr_s = jnp.transpose(gc_s, (0, 2, 1))
        col = lambda a: jnp.transpose(a.reshape(sbsz, s_len, MLSTM_HEADS, MLSTM_DK), (0, 2, 3, 1))
        oa_s = swa_sample(l, attn_sinks[l], qa_s, ka_s, va_s, swa_kt, swa_vt)
        od_s = dil_sample(l, qd_s, kd_s, vd_s, dil_kt, dil_vt)
        omx_s, sc, sn, sm = mlstm_sample(l, qm_s, km_s, vm_s, col(qm_s), col(km_s), om_s, gc_s, gr_s,
                                         bias_row, bias_col, norm_w, state_mlstm_c, n_in, m_in)
        flat = lambda a: a.reshape(n_s, a.shape[2]).astype(BF16)

        x1, e_ids, gates = out_proj(oa_p, od_p, om_p, flat(oa_s), flat(od_s), flat(omx_s), x,
                                    w_out[l].astype(BF16), ln1_g[l][None, :], ln1_b[l][None, :], rw_t, rb_col)
        y = moe_experts(x1, *moe_plan(e_ids, gates, n_tok), wg_b[l], wu_b[l], wd_b[l])
        x = moe_combine_ln2(x1, y, ln2_g[l][None, :], ln2_b[l][None, :])

        heads = lambda a, h: a[:n_p].reshape(bsz, t_len, h, HEAD_DIM)
        swa_keep, dil_keep = min(SWA_WINDOW, t_len), min(DIL_WINDOW_MAX, t_len)
        p_states.append((heads(ka, SWA_KV_HEADS)[:, t_len - swa_keep:], heads(va, SWA_KV_HEADS)[:, t_len - swa_keep:],
                         heads(kd, DIL_HEADS)[:, t_len - dil_keep:], heads(vd, DIL_HEADS)[:, t_len - dil_keep:],
                         pc, pn[:, :, 0, :], pm[:, :, 0, 0]))
        s_states.append((sc, sn[:, :, 0, :], sm[:, :, 0, 0]))
        new_t = lambda a, h: jnp.transpose(a.reshape(sbsz, s_len, h, HEAD_DIM), (0, 2, 3, 1))
        new_cols.append((new_t(ka_s, SWA_KV_HEADS), new_t(va_s, SWA_KV_HEADS), new_t(kd_s, DIL_HEADS), new_t(vd_s, DIL_HEADS)))

    p_swa_k, p_swa_v, p_dil_k, p_dil_v, p_c, p_n, p_m = [jnp.stack(a) for a in zip(*p_states)]
    s_c, s_n, s_m = [jnp.stack(a) for a in zip(*s_states)]
    nk_swa, nv_swa, nk_dil, nv_dil = [jnp.stack(a) for a in zip(*new_cols)]
    s_swa_kt, s_swa_vt = cache_roll(swa_kt, swa_vt, nk_swa, nv_swa, bt=16)
    s_dil_kt, s_dil_vt = cache_roll(dil_kt, dil_vt, nk_dil, nv_dil, bt=1)
    return (x[:n_p].reshape(bsz, t_len, d), x[n_p:].reshape(sbsz, s_len, d),
            p_swa_k, p_swa_v, p_dil_k, p_dil_v, p_c, p_n, p_m,
            from_lanes(s_swa_kt), from_lanes(s_swa_vt), from_lanes(s_dil_kt), from_lanes(s_dil_vt), s_c, s_n, s_m)
```

```python
import functools

import jax
import jax.numpy as jnp
from jax import lax
from jax.experimental import pallas as pl
from jax.experimental.pallas import tpu as pltpu

F32 = jnp.float32
BF16 = jnp.bfloat16
I32 = jnp.int32

D_MODEL = 1024
DEPTH = 4
HEAD_DIM = 64
SWA_HEADS = 4
SWA_KV_HEADS = 2
SWA_WINDOW = 128
DIL_HEADS = 4
DIL_PAIRS = ((128, 1), (512, 4), (2048, 16))
DIL_WINDOW_MAX = 2048
MLSTM_HEADS = 4
MLSTM_DK = 128
MLSTM_DV = 128
ROPE_DIM = HEAD_DIM // 4
ROPE_THETA = 500000.0
N_EXPERTS = 16
N_EXPERT_GROUPS = 4
EXPERTS_PER_GROUP = 4
TOP_K = 2
D_EXPERT = 512
LN_EPS = 1e-5
DEEPNORM_ALPHA = (2.0 * DEPTH) ** 0.25

_QA, _KA, _VA, _QD, _KD, _VD, _QM, _KM, _VM, _OM, _GATES = 0, 256, 384, 512, 768, 1024, 1280, 1792, 2304, 2816, 3328
_MAIN_COLS = 3328

LANES = 128
SUBLANES = 8
VMEM_LIMIT = 56 * 1024 * 1024


def _cparams(sem):
    return pltpu.CompilerParams(dimension_semantics=sem, vmem_limit_bytes=VMEM_LIMIT)


def _in_proj_kernel(x_ref, w_ref, wg_ref, wgt_ref, cos_ref, sa_ref, sb_ref,
                    qa_ref, ka_ref, va_ref, qd_ref, kd_ref, vd_ref,
                    qm_ref, km_ref, vm_ref, om_ref, gc_ref, gr_ref):
    xb = x_ref[...].astype(BF16)
    cos = cos_ref[...]
    sa = sa_ref[...]
    sb = sb_ref[...]

    def seg(c0, n):
        return jnp.dot(xb, w_ref[:, c0:c0 + n], preferred_element_type=F32)

    def rope_store(out_ref, c0, n):
        for j in range(n // LANES):
            v = seg(c0 + j * LANES, LANES)
            r = v * cos + pltpu.roll(v, LANES - ROPE_DIM // 2, 1) * sa + pltpu.roll(v, ROPE_DIM // 2, 1) * sb
            out_ref[:, j * LANES:(j + 1) * LANES] = r.astype(out_ref.dtype)

    rope_store(qa_ref, _QA, 256)
    rope_store(ka_ref, _KA, 128)
    va_ref[...] = seg(_VA, 128)
    rope_store(qd_ref, _QD, 256)
    rope_store(kd_ref, _KD, 256)
    vd_ref[...] = seg(_VD, 256)
    qm_ref[...] = (seg(_QM, 512) * (MLSTM_DK ** -0.5)).astype(BF16)
    km_ref[...] = seg(_KM, 512)
    vm_ref[...] = seg(_VM, 512).astype(BF16)
    om_ref[...] = seg(_OM, 512)
    gc_ref[...] = jnp.dot(xb, wg_ref[...], preferred_element_type=F32)
    gr_ref[...] = lax.dot_general(wgt_ref[...], xb, (((1,), (1,)), ((), ())), preferred_element_type=F32)


def in_proj(x, w_main, w_g, w_gt, cos, sa, sb, tm=512):
    nt = x.shape[0]
    row = lambda n: pl.BlockSpec((tm, n), lambda i: (i, 0))
    full = lambda a: pl.BlockSpec(a.shape, lambda i: (0, 0))
    outs = [(256, BF16), (128, F32), (128, F32), (256, F32), (256, F32), (256, F32),
            (512, BF16), (512, F32), (512, BF16), (512, F32), (8, F32)]
    out_shape = [jax.ShapeDtypeStruct((nt, n), dt) for n, dt in outs] + [jax.ShapeDtypeStruct((8, nt), F32)]
    out_specs = [row(n) for n, _ in outs] + [pl.BlockSpec((8, tm), lambda i: (0, i))]
    return pl.pallas_call(
        _in_proj_kernel,
        grid=(nt // tm,),
        in_specs=[row(D_MODEL), full(w_main), full(w_g), full(w_gt), row(LANES), row(LANES), row(LANES)],
        out_specs=out_specs,
        out_shape=out_shape,
        compiler_params=_cparams(("parallel",)),
        name="in_proj",
    )(x, w_main, w_g, w_gt, cos, sa, sb)


def rope_tables(pos):
    half = ROPE_DIM // 2
    inv_freq = ROPE_THETA ** (-jnp.arange(half, dtype=F32) / half)
    ang = pos.astype(F32)[:, None] * inv_freq[None, :]
    c, s = jnp.cos(ang), jnp.sin(ang)
    n = pos.shape[0]
    one = jnp.ones((n, HEAD_DIM - ROPE_DIM), F32)
    zero = jnp.zeros((n, HEAD_DIM - ROPE_DIM), F32)
    zh = jnp.zeros((n, half), F32)
    cos64 = jnp.concatenate([c, c, one], axis=1)
    sa64 = jnp.concatenate([-s, zh, zero], axis=1)
    sb64 = jnp.concatenate([zh, s, zero], axis=1)
    tile2 = lambda a: jnp.concatenate([a, a], axis=1)
    return tile2(cos64), tile2(sa64), tile2(sb64)


SWA_HEAD_ORDER = (0, 3, 1, 2)
SWA_BLOCK = 128
SWA_STEP = 512


def _swa_prompt_kernel(sink_ref, q_ref, kp_ref, kc_ref, vp_ref, vc_ref, o_ref):
    i = pl.program_id(1)
    n = SWA_BLOCK
    r = lax.broadcasted_iota(I32, (n, 2 * n), 0)
    c = lax.broadcasted_iota(I32, (n, 2 * n), 1)
    band = ((c < n) & (c >= r)) | ((c >= n) & (c - n <= r))
    lane = lax.broadcasted_iota(I32, (n, LANES), 1)
    low = lane < HEAD_DIM
    scale = HEAD_DIM ** -0.5
    nt = (((1,), (1,)), ((), ()))
    for j in range(SWA_STEP // n):
        rows = slice(j * n, (j + 1) * n)
        if j == 0:
            kk = jnp.concatenate([kp_ref[...], kc_ref[rows, :]], axis=0)
            vv = jnp.concatenate([vp_ref[...], vc_ref[rows, :]], axis=0)
            mask = band & ((c >= n) | (i > 0))
        else:
            kk = kc_ref[(j - 1) * n:(j + 1) * n, :]
            vv = vc_ref[(j - 1) * n:(j + 1) * n, :]
            mask = band
        kb, vb = kk.astype(BF16), vv.astype(BF16)
        for col in range(SWA_HEADS * HEAD_DIM // LANES):
            q = q_ref[rows, col * LANES:(col + 1) * LANES]
            halves = []
            for half in range(LANES // HEAD_DIM):
                own = low if half == 0 else ~low
                qh = jnp.where(own, q, jnp.zeros_like(q))
                s = jnp.where(mask, lax.dot_general(qh, kb, nt, preferred_element_type=F32) * scale, -jnp.inf)
                sink = sink_ref[SWA_HEAD_ORDER[col * (LANES // HEAD_DIM) + half]]
                m = jnp.maximum(s.max(-1, keepdims=True), sink)
                p = jnp.exp(s - m)
                p = p / (p.sum(-1, keepdims=True) + jnp.exp(sink - m))
                halves.append(jnp.dot(p.astype(BF16), vb, preferred_element_type=F32))
            o_ref[rows, col * LANES:(col + 1) * LANES] = jnp.where(low, halves[0], halves[1]).astype(o_ref.dtype)


def swa_prompt(sinks, qa, ka, va, bsz, t_len):
    nb = t_len // SWA_STEP
    per = SWA_STEP // SWA_BLOCK
    cur = lambda b, i: (b * nb + i, 0)
    prev = lambda b, i: (jnp.maximum((b * nb + i) * per - 1, 0), 0)
    kv = SWA_KV_HEADS * HEAD_DIM
    w = SWA_HEADS * HEAD_DIM
    return pl.pallas_call(
        _swa_prompt_kernel,
        grid=(bsz, nb),
        in_specs=[pl.BlockSpec(memory_space=pltpu.SMEM),
                  pl.BlockSpec((SWA_STEP, w), cur),
                  pl.BlockSpec((SWA_BLOCK, kv), prev), pl.BlockSpec((SWA_STEP, kv), cur),
                  pl.BlockSpec((SWA_BLOCK, kv), prev), pl.BlockSpec((SWA_STEP, kv), cur)],
        out_specs=pl.BlockSpec((SWA_STEP, w), cur),
        out_shape=jax.ShapeDtypeStruct((bsz * t_len, w), BF16),
        compiler_params=_cparams(("parallel", "parallel")),
        name="swa_prompt",
    )(sinks, qa, ka, ka, va, va)


DIL_TILE = 2048
DIL_BLOCK = 128


def _dil_scores(qv, kk, prev_valid):
    n = DIL_BLOCK
    r = lax.broadcasted_iota(I32, (n, 2 * n), 0)
    c = lax.broadcasted_iota(I32, (n, 2 * n), 1)
    mask = ((c < n) & (c >= r) & prev_valid) | ((c >= n) & (c - n <= r))
    lane = lax.broadcasted_iota(I32, (n, LANES), 1)
    scale = HEAD_DIM ** -0.5
    out = []
    for h in range(LANES // HEAD_DIM):
        own = (lane >= h * HEAD_DIM) & (lane < (h + 1) * HEAD_DIM)
        qh = jnp.where(own, qv, 0.0)
        s = lax.dot_general(qh, kk, (((1,), (1,)), ((), ())), precision=_HI, preferred_element_type=F32) * scale
        out.append((own, jnp.where(mask, s, -jnp.inf)))
    return out


def _pick_head(parts):
    return jnp.where(parts[0][0], parts[0][1], parts[1][1])


def _dil_prompt_kernel(q_ref, kp_ref, kc_ref, vp_ref, vc_ref, o_ref, *scratch):
    has_prev_tile = pl.program_id(1) > 0
    n = DIL_BLOCK
    n_pat = len(DIL_PAIRS)
    a_refs, m_refs, l_refs = scratch[:n_pat], scratch[n_pat:2 * n_pat], scratch[2 * n_pat:3 * n_pat]
    max_ref = scratch[3 * n_pat]

    def sweep(unit):
        for pi, (_, d) in enumerate(DIL_PAIRS):
            span = n * d
            for res in range(d):
                rows0 = pl.ds(res, n, stride=d) if d > 1 else pl.ds(0, n)
                prow = pl.ds(DIL_TILE - span + res, n, stride=d) if d > 1 else pl.ds(DIL_TILE - n, n)

                def get0(ref, rows0=rows0):
                    return ref[rows0, :]

                def keys0(prev_ref, cur_ref, rows0=rows0, prow=prow):
                    return jnp.concatenate([prev_ref[prow, :], cur_ref[rows0, :]], axis=0)

                def put0(ref, val, rows0=rows0):
                    ref[rows0, :] = val
                unit(pi, get0, keys0, put0, has_prev_tile)

                n_blk = DIL_TILE // span
                if n_blk > 1:
                    def body(j, carry, pi=pi, d=d, res=res, span=span):
                        win = pl.ds(pl.multiple_of((j - 1) * span, span), 2 * span)
                        both = pl.ds(res, 2 * n, stride=d) if d > 1 else pl.ds(0, 2 * n)
                        crow = pl.ds(span + res, n, stride=d) if d > 1 else pl.ds(n, n)

                        def get(ref):
                            return ref.at[win, :][crow, :]

                        def keys(prev_ref, cur_ref):
                            return cur_ref.at[win, :][both, :]

                        def put(ref, val):
                            ref.at[win, :][crow, :] = val
                        unit(pi, get, keys, put, True)
                        return carry
                    lax.fori_loop(1, n_blk, body, 0, unroll=3 if (n_blk - 1) % 3 == 0 else 1)

    def find_max(pi, get, keys, put, prev_valid):
        parts = _dil_scores(get(q_ref), keys(kp_ref, kc_ref), prev_valid)
        put(m_refs[pi], _pick_head([(own, s.max(-1, keepdims=True)) for own, s in parts]))

    sweep(find_max)
    max_ref[...] = functools.reduce(jnp.maximum, [m[...] for m in m_refs])

    def accumulate(pi, get, keys, put, prev_valid):
        parts = _dil_scores(get(q_ref), keys(kp_ref, kc_ref), prev_valid)
        vv = keys(vp_ref, vc_ref)
        m_all = get(max_ref)
        acc, den = [], []
        for h, (own, s) in enumerate(parts):
            e = jnp.exp(s - m_all[:, h * HEAD_DIM:h * HEAD_DIM + 1])
            den.append((own, e.sum(-1, keepdims=True)))
            acc.append((own, jnp.dot(e, vv, precision=_HI, preferred_element_type=F32)))
        put(a_refs[pi], _pick_head(acc))
        put(l_refs[pi], _pick_head(den))

    sweep(accumulate)

    chunk = 256

    def merge(i, carry):
        rows = pl.ds(pl.multiple_of(i * chunk, chunk), chunk)
        dens = [l[rows, :] for l in l_refs]
        num = functools.reduce(jnp.add, [den * (a[rows, :] / den) for a, den in zip(a_refs, dens)])
        o_ref[rows, :] = (num / functools.reduce(jnp.add, dens)).astype(o_ref.dtype)
        return carry
    lax.fori_loop(0, DIL_TILE // chunk, merge, 0)


def dil_prompt(qd, kd, vd, bsz, t_len):
    ntile = t_len // DIL_TILE
    cur = lambda b, i, p: (b * ntile + i, p)
    prev = lambda b, i, p: (b * ntile + jnp.maximum(i - 1, 0), p)
    w = DIL_HEADS * HEAD_DIM
    blk = lambda im: pl.BlockSpec((DIL_TILE, LANES), im)
    return pl.pallas_call(
        _dil_prompt_kernel,
        grid=(bsz, ntile, w // LANES),
        in_specs=[blk(cur), blk(prev), blk(cur), blk(prev), blk(cur)],
        out_specs=blk(cur),
        out_shape=jax.ShapeDtypeStruct((bsz * t_len, w), BF16),
        scratch_shapes=[pltpu.VMEM((DIL_TILE, LANES), F32)] * (3 * len(DIL_PAIRS) + 1),
        compiler_params=_cparams(("parallel", "parallel", "parallel")),
        name="dil_prompt",
    )(qd, kd, kd, vd, vd)


MLSTM_CHUNK = 64
MLSTM_STEP = 128
_HI = lax.Precision.HIGHEST


def _log_sigmoid(x):
    return jnp.minimum(x, 0.0) - jnp.log1p(jnp.exp(-jnp.abs(x)))


def _mlstm_prompt_kernel(n_seq, *refs):
    ins, (brow_ref, bcol_ref, nw_ref) = refs[:6 * n_seq], refs[6 * n_seq:6 * n_seq + 3]
    o_ref, c_ref, n_ref, m_ref = refs[6 * n_seq + 3:]
    q_refs, k_refs, v_refs, om_refs, gc_refs, gr_refs = [ins[i * n_seq:(i + 1) * n_seq] for i in range(6)]
    L = MLSTM_CHUNK

    @pl.when(pl.program_id(0) == 0)
    def _():
        c_ref[...] = jnp.zeros_like(c_ref)
        n_ref[...] = jnp.zeros_like(n_ref)
        m_ref[...] = jnp.zeros_like(m_ref)

    r = lax.broadcasted_iota(I32, (L, L), 0)
    c = lax.broadcasted_iota(I32, (L, L), 1)
    causal = c <= r
    tril = causal.astype(F32)
    triu = (r <= c).astype(F32)
    nt = (((1,), (1,)), ((), ()))
    H = MLSTM_HEADS
    for sub in range(MLSTM_STEP // L):
        rows = slice(sub * L, (sub + 1) * L)
        for b in range(n_seq):
            g_col = gc_refs[b][rows, :] + brow_ref[...]
            g_row = gr_refs[b][:, rows] + bcol_ref[...]
            bcum_col = jnp.dot(tril, _log_sigmoid(g_col), precision=_HI, preferred_element_type=F32)
            bcum_row = jnp.dot(_log_sigmoid(g_row), triu, precision=_HI, preferred_element_type=F32)
            for h in range(H):
                sl = slice(h * MLSTM_DK, (h + 1) * MLSTM_DK)
                ig_col, ig_row = g_col[:, h:h + 1], g_row[h:h + 1, :]
                b_col, b_row = bcum_col[:, H + h:H + h + 1], bcum_row[H + h:H + h + 1, :]
                m_prev = m_ref[b, h, :, 0:1]
                c_prev = c_ref[b, h]
                n_prev = n_ref[b, h]
                q, kf, v = q_refs[b][rows, sl], k_refs[b][rows, sl], v_refs[b][rows, sl]
                k = kf.astype(BF16)
                n_b = n_prev.astype(BF16).astype(F32)

                log_d = jnp.where(causal, b_col - b_row + ig_row, -jnp.inf)
                log_inter = b_col + m_prev
                m_t = jnp.maximum(log_inter, log_d.max(-1, keepdims=True))
                w_ts = jnp.exp(log_d - m_t) * lax.dot_general(q, k, nt, preferred_element_type=F32)
                w_inter = jnp.exp(log_inter - m_t)
                num = (jnp.dot(w_ts.astype(BF16), v, preferred_element_type=F32)
                       + w_inter * jnp.dot(q, c_prev.astype(BF16), preferred_element_type=F32))
                den = w_ts.sum(-1, keepdims=True) + w_inter * (q.astype(F32) * n_b).sum(-1, keepdims=True)
                hh = num / jnp.maximum(jnp.abs(den), jnp.exp(-m_t))

                mu = hh.mean(-1, keepdims=True)
                var = jnp.mean(jnp.square(hh - mu), -1, keepdims=True)
                hn = (hh - mu) * lax.rsqrt(var + LN_EPS) * nw_ref[:, sl]
                o_ref[b, rows, sl] = (hn * jax.nn.sigmoid(om_refs[b][rows, sl])).astype(o_ref.dtype)

                b_last = b_col[L - 1:L, :]
                m_new = jnp.maximum(b_last + m_prev, (b_last - b_row + ig_row).max(-1, keepdims=True))
                w_s = jnp.exp(b_last - b_col + ig_col - m_new)
                decay = jnp.exp(b_last + m_prev - m_new)
                c_ref[b, h] = decay * c_prev + jnp.dot((w_s * kf).T.astype(BF16), v, preferred_element_type=F32)
                n_ref[b, h] = decay * n_prev + (w_s.astype(BF16).astype(F32) * k.astype(F32)).sum(0, keepdims=True)
                m_ref[b, h] = jnp.broadcast_to(m_new, (1, LANES))


def mlstm_prompt(qm, km, vm, om, gc, gr, bias_row, bias_col, norm_w, bsz, t_len):
    S = MLSTM_STEP
    ns = t_len // S
    H = MLSTM_HEADS
    w = H * MLSTM_DV
    rows = lambda n: [pl.BlockSpec((S, n), lambda j, b=b: (b * ns + j, 0)) for b in range(bsz)]
    cols = [pl.BlockSpec((8, S), lambda j, b=b: (0, b * ns + j)) for b in range(bsz)]
    const = lambda shape: pl.BlockSpec(shape, lambda j: (0,) * len(shape))
    st = lambda *tail: pl.BlockSpec((bsz, H) + tail, lambda j: (0,) * (2 + len(tail)))
    args = [qm] * bsz + [km] * bsz + [vm] * bsz + [om] * bsz + [gc] * bsz + [gr] * bsz
    return pl.pallas_call(
        functools.partial(_mlstm_prompt_kernel, bsz),
        grid=(ns,),
        in_specs=rows(w) + rows(w) + rows(w) + rows(w) + rows(8) + cols + [const((1, 8)), const((8, 1)), const((1, w))],
        out_specs=[pl.BlockSpec((bsz, S, w), lambda j: (0, j, 0)), st(MLSTM_DK, MLSTM_DV), st(1, MLSTM_DK), st(1, LANES)],
        out_shape=[jax.ShapeDtypeStruct((bsz, t_len, w), BF16),
                   jax.ShapeDtypeStruct((bsz, H, MLSTM_DK, MLSTM_DV), F32),
                   jax.ShapeDtypeStruct((bsz, H, 1, MLSTM_DK), F32),
                   jax.ShapeDtypeStruct((bsz, H, 1, LANES), F32)],
        compiler_params=_cparams(("arbitrary",)),
        name="mlstm_prompt",
    )(*args, bias_row, bias_col, norm_w)


ROW_TILES = D_MODEL // LANES


def _store_row_tiles(ref, val):
    n = val.shape[0]
    for s in range(ROW_TILES):
        ref[pl.ds(s, n, stride=ROW_TILES), :] = val[:, s * LANES:(s + 1) * LANES]


def _load_row_tiles(ref, n):
    return jnp.concatenate([ref[pl.ds(s, n, stride=ROW_TILES), :] for s in range(ROW_TILES)], axis=1)


def _layer_norm(y, g, b):
    mu = y.mean(-1, keepdims=True)
    var = jnp.mean(jnp.square(y - mu), -1, keepdims=True)
    return (y - mu) * lax.rsqrt(var + LN_EPS) * g + b


def _group_partner(a, j, sub):
    up = pltpu.roll(a, N_EXPERTS - j, 0)
    down = pltpu.roll(a, EXPERTS_PER_GROUP - j, 0)
    return jnp.where(sub + j < EXPERTS_PER_GROUP, up, down)


def _route(logits_t):
    n = logits_t.shape[1]
    e_id = lax.broadcasted_iota(I32, (N_EXPERTS, n), 0)
    sub = e_id % EXPERTS_PER_GROUP
    grp = e_id // EXPERTS_PER_GROUP
    ex = jnp.exp(logits_t - logits_t.max(0, keepdims=True))
    aff = ex / ex.sum(0, keepdims=True)
    rank = jnp.zeros((N_EXPERTS, n), I32)
    for j in range(1, EXPERTS_PER_GROUP):
        other = _group_partner(aff, j, sub)
        other_sub = (sub + j) % EXPERTS_PER_GROUP
        rank += ((other > aff) | ((other == aff) & (other_sub < sub))).astype(I32)
    top = jnp.where(rank < TOP_K, aff, 0.0)
    score = top
    for j in range(1, EXPERTS_PER_GROUP):
        score = score + _group_partner(top, j, sub)
    grank = jnp.zeros((N_EXPERTS, n), I32)
    for j in range(1, N_EXPERT_GROUPS):
        other = pltpu.roll(score, N_EXPERTS - EXPERTS_PER_GROUP * j, 0)
        other_grp = (grp + j) % N_EXPERT_GROUPS
        grank += ((other > score) | ((other == score) & (other_grp < grp))).astype(I32)
    chosen = grank == 0
    ids, vals = [], []
    for slot in range(TOP_K):
        pick = chosen & (rank == slot)
        ids.append(jnp.where(pick, e_id, 0).sum(0, keepdims=True))
        vals.append(jnp.where(pick, aff, 0.0).sum(0, keepdims=True))
    total = vals[0] + vals[1]
    return ids, [v / total for v in vals]


def _out_proj_kernel(n_prompt_tiles, oap_ref, odp_ref, omp_ref, oas_ref, ods_ref, oms_ref, x_ref, w_ref,
                     g_ref, b_ref, rwt_ref, rb_ref, x1_ref, e_ref, gate_ref):
    is_prompt = pl.program_id(0) < n_prompt_tiles
    pick = lambda p_ref, s_ref: jnp.where(is_prompt, p_ref[...], s_ref[...])
    o_a, o_d, o_m = pick(oap_ref, oas_ref), pick(odp_ref, ods_ref), pick(omp_ref, oms_ref)
    na, nd = o_a.shape[1], o_d.shape[1]
    mix = (jnp.dot(o_a, w_ref[0:na, :], preferred_element_type=F32)
           + jnp.dot(o_d, w_ref[na:na + nd, :], preferred_element_type=F32)
           + jnp.dot(o_m, w_ref[na + nd:, :], preferred_element_type=F32))
    x1 = _layer_norm(DEEPNORM_ALPHA * x_ref[...] + mix, g_ref[...], b_ref[...])
    _store_row_tiles(x1_ref, x1)
    logits_t = lax.dot_general(rwt_ref[...].astype(BF16), x1.astype(BF16), (((1,), (1,)), ((), ())),
                               preferred_element_type=F32) + rb_ref[...]
    ids, gates = _route(logits_t)
    for slot in range(TOP_K):
        e_ref[slot:slot + 1, :] = ids[slot]
        gate_ref[slot:slot + 1, :] = gates[slot]


def out_proj(oa_p, od_p, om_p, oa_s, od_s, om_s, x, w_out, ln_g, ln_b, rw_t, rb_col, tm=512):
    nt = x.shape[0]
    n_p = oa_p.shape[0]
    assert n_p % tm == 0 and oa_s.shape[0] == tm and nt == n_p + tm
    npt = n_p // tm
    prow = lambda n: pl.BlockSpec((tm, n), lambda i: (jnp.minimum(i, npt - 1), 0))
    srow = lambda n: pl.BlockSpec((tm, n), lambda i: (0, 0))
    row = lambda n: pl.BlockSpec((tm, n), lambda i: (i, 0))
    full = lambda a: pl.BlockSpec(a.shape, lambda i: (0, 0))
    col = pl.BlockSpec((TOP_K, tm), lambda i: (0, i))
    return pl.pallas_call(
        functools.partial(_out_proj_kernel, npt),
        grid=(nt // tm,),
        in_specs=[prow(256), prow(256), prow(512), srow(256), srow(256), srow(512), row(D_MODEL), full(w_out),
                  full(ln_g), full(ln_b), full(rw_t), full(rb_col)],
        out_specs=[pl.BlockSpec((tm * ROW_TILES, LANES), lambda i: (i, 0)), col, col],
        out_shape=[jax.ShapeDtypeStruct((nt * ROW_TILES, LANES), F32), jax.ShapeDtypeStruct((TOP_K, nt), I32),
                   jax.ShapeDtypeStruct((TOP_K, nt), F32)],
        compiler_params=_cparams(("parallel",)),
        name="out_proj",
    )(oa_p, od_p, om_p, oa_s, od_s, om_s, x, w_out, ln_g, ln_b, rw_t, rb_col)


MOE_ROWS = 256


def moe_plan(e_ids, n_tok):
    n_assign = TOP_K * n_tok
    n_blocks = -(-n_assign // MOE_ROWS) + N_EXPERTS
    flat_e = e_ids.reshape(-1)
    onehot = (flat_e[:, None] == jnp.arange(N_EXPERTS, dtype=I32)[None, :]).astype(I32)
    csum = jnp.cumsum(onehot, axis=0)
    rank = jnp.take_along_axis(csum, flat_e[:, None], axis=1)[:, 0] - 1
    counts = csum[-1]
    padded = (counts + MOE_ROWS - 1) // MOE_ROWS * MOE_ROWS
    pad_end = jnp.cumsum(padded)
    dest = (pad_end - padded)[flat_e] + rank
    blk_e = jnp.clip(jnp.searchsorted(pad_end, jnp.arange(n_blocks, dtype=I32) * MOE_ROWS, side='right'),
                     0, N_EXPERTS - 1).astype(I32)
    n_used = (pad_end[-1] // MOE_ROWS).astype(I32).reshape(1)
    n_rows = n_blocks * MOE_ROWS
    spare = n_assign + jnp.arange(n_rows, dtype=I32) % MOE_ROWS
    row_dst = spare.at[dest].set(jnp.arange(n_assign, dtype=I32))
    row_tok = jnp.where(row_dst >= n_assign, 0, row_dst % n_tok)
    return blk_e, n_used, row_dst, row_tok


def _moe_kernel(n_tok, blk_e_ref, n_used_ref, row_dst_ref, row_tok_ref, x_hbm, wg_ref, wu_ref, wd_ref, y_hbm,
                xbuf, ybuf, wg_b, wu_b, wd_b, sem_g, sem_s):
    i = pl.program_id(0)
    R = MOE_ROWS
    n_used = n_used_ref[0]
    slot = i % 2

    tile = lambda row: pl.ds(pl.multiple_of(row * ROW_TILES, ROW_TILES), ROW_TILES)

    def start_gather(blk, s):
        def body(r, carry):
            tok = row_tok_ref[blk * R + r]
            pltpu.make_async_copy(x_hbm.at[tile(tok), :], xbuf.at[s, tile(r), :], sem_g.at[s]).start()
            return carry
        lax.fori_loop(0, R, body, 0, unroll=8)

    def start_scatter(blk, s):
        def body(r, carry):
            d = row_dst_ref[blk * R + r]
            pltpu.make_async_copy(ybuf.at[s, tile(r), :], y_hbm.at[tile(d), :], sem_s.at[s]).start()
            return carry
        lax.fori_loop(0, R, body, 0, unroll=8)

    block = lambda hbm, row0: hbm.at[pl.ds(row0 * ROW_TILES, R * ROW_TILES), :]
    wait_gather = lambda s: pltpu.make_async_copy(block(x_hbm, 0), xbuf.at[s], sem_g.at[s]).wait()
    wait_scatter = lambda s: pltpu.make_async_copy(ybuf.at[s], block(y_hbm, 0), sem_s.at[s]).wait()

    @pl.when(i == 0)
    def _():
        ybuf[1] = jnp.zeros(ybuf.shape[1:], F32)
        spare = pltpu.make_async_copy(ybuf.at[1], block(y_hbm, TOP_K * n_tok), sem_s.at[1])
        spare.start()
        spare.wait()

        @pl.when(n_used > 0)
        def _():
            start_gather(0, 0)

    @pl.when(i + 1 < n_used)
    def _():
        start_gather(i + 1, 1 - slot)

    @pl.when(i < n_used)
    def _():
        @pl.when((i == 0) | (blk_e_ref[i] != blk_e_ref[jnp.maximum(i - 1, 0)]))
        def _():
            wg_b[...] = wg_ref[0, 0].astype(BF16)
            wu_b[...] = wu_ref[0, 0].astype(BF16)
            wd_b[...] = wd_ref[0, 0].astype(BF16)

        wait_gather(slot)
        xb = _load_row_tiles(xbuf.at[slot], R).astype(BF16)
        hid = (jax.nn.silu(jnp.dot(xb, wg_b[...], preferred_element_type=F32))
               * jnp.dot(xb, wu_b[...], preferred_element_type=F32))
        _store_row_tiles(ybuf.at[slot], jnp.dot(hid.astype(BF16), wd_b[...], preferred_element_type=F32))
        start_scatter(i, slot)

        @pl.when(i > 0)
        def _():
            wait_scatter(1 - slot)

        @pl.when(i == n_used - 1)
        def _():
            wait_scatter(slot)


def moe_experts(layer, x1t, blk_e, n_used, row_dst, row_tok, w_gate, w_up, w_down):
    n_tok = x1t.shape[0] // ROW_TILES
    n_blocks = blk_e.shape[0]
    R = MOE_ROWS
    wspec = lambda shape: pl.BlockSpec((1, 1) + shape, lambda i, be, nu, rd, rt: (layer, be[i], 0, 0))
    grid_spec = pltpu.PrefetchScalarGridSpec(
        num_scalar_prefetch=4,
        grid=(n_blocks,),
        in_specs=[pl.BlockSpec(memory_space=pl.ANY),
                  wspec((D_MODEL, D_EXPERT)), wspec((D_MODEL, D_EXPERT)), wspec((D_EXPERT, D_MODEL))],
        out_specs=pl.BlockSpec(memory_space=pl.ANY),
        scratch_shapes=[pltpu.VMEM((2, R * ROW_TILES, LANES), F32), pltpu.VMEM((2, R * ROW_TILES, LANES), F32),
                        pltpu.VMEM((D_MODEL, D_EXPERT), BF16), pltpu.VMEM((D_MODEL, D_EXPERT), BF16),
                        pltpu.VMEM((D_EXPERT, D_MODEL), BF16),
                        pltpu.SemaphoreType.DMA((2,)), pltpu.SemaphoreType.DMA((2,))],
    )
    return pl.pallas_call(
        functools.partial(_moe_kernel, n_tok),
        grid_spec=grid_spec,
        out_shape=jax.ShapeDtypeStruct(((TOP_K * n_tok + R) * ROW_TILES, LANES), F32),
        compiler_params=_cparams(("arbitrary",)),
        name="moe_experts",
    )(blk_e, n_used, row_dst, row_tok, x1t, w_gate, w_up, w_down)


def _ln2_kernel(x1_ref, y0_ref, y1_ref, gate_ref, g_ref, b_ref, o_ref):
    tm = o_ref.shape[0]
    y = gate_ref[:, 0:1] * _load_row_tiles(y0_ref, tm) + gate_ref[:, 1:2] * _load_row_tiles(y1_ref, tm)
    o_ref[...] = _layer_norm(DEEPNORM_ALPHA * _load_row_tiles(x1_ref, tm) + y, g_ref[...], b_ref[...])


def moe_combine_ln2(x1t, y, gates_t, ln_g, ln_b, tm=512):
    nt = x1t.shape[0] // ROW_TILES
    nb = nt // tm
    row = lambda off: pl.BlockSpec((tm * ROW_TILES, LANES), lambda i: (i + off, 0))
    full = lambda a: pl.BlockSpec(a.shape, lambda i: (0, 0))
    return pl.pallas_call(
        _ln2_kernel,
        grid=(nb,),
        in_specs=[row(0), row(0), row(nb), pl.BlockSpec((tm, TOP_K), lambda i: (i, 0)), full(ln_g), full(ln_b)],
        out_specs=pl.BlockSpec((tm, D_MODEL), lambda i: (i, 0)),
        out_shape=jax.ShapeDtypeStruct((nt, D_MODEL), F32),
        compiler_params=_cparams(("parallel",)),
        name="moe_combine_ln2",
    )(x1t, y, y, gates_t, ln_g, ln_b)


def _dil_counts(n_new, n_buf, col0, n_col):
    i = lax.broadcasted_iota(I32, (n_new, n_col), 0)
    c = lax.broadcasted_iota(I32, (n_new, n_col), 1) + col0
    delta = n_buf + i - c
    cnt = jnp.zeros((n_new, n_col), F32)
    for window, dil in DIL_PAIRS:
        cnt += ((delta >= 0) & (delta <= window) & (delta % dil == 0)).astype(F32)
    return cnt


def _dil_sample_kernel(q_ref, kn_ref, vn_ref, kt_ref, vt_ref, o_ref):
    bt, n_new = q_ref.shape[0], q_ref.shape[1]
    n_buf = kt_ref.shape[-1]
    cnt_c = _dil_counts(n_new, n_buf, 0, n_buf)
    cnt_n = _dil_counts(n_new, n_buf, n_buf, n_new)
    scale = HEAD_DIM ** -0.5
    nt = (((1,), (1,)), ((), ()))

    def body(b, carry):
        q, kn, vn = q_ref[b], kn_ref[b], vn_ref[b]
        for h in range(DIL_HEADS):
            sl = slice(h * HEAD_DIM, (h + 1) * HEAD_DIM)
            qh = q[:, sl]
            kt = kt_ref[0, b, h]
            vt = vt_ref[0, b, h]
            s_c = jnp.dot(qh, kt, precision=_HI, preferred_element_type=F32) * scale
            s_n = lax.dot_general(qh, kn[:, sl], nt, precision=_HI, preferred_element_type=F32) * scale
            m = jnp.maximum(jnp.where(cnt_c > 0, s_c, -jnp.inf).max(-1, keepdims=True),
                            jnp.where(cnt_n > 0, s_n, -jnp.inf).max(-1, keepdims=True))
            e_c = cnt_c * jnp.exp(jnp.where(cnt_c > 0, s_c - m, -jnp.inf))
            e_n = cnt_n * jnp.exp(jnp.where(cnt_n > 0, s_n - m, -jnp.inf))
            den = e_c.sum(-1, keepdims=True) + e_n.sum(-1, keepdims=True)
            o = (lax.dot_general(e_c, vt, nt, precision=_HI, preferred_element_type=F32)
                 + jnp.dot(e_n, vn[:, sl], precision=_HI, preferred_element_type=F32))
            o_ref[b, :, sl] = o / den
        return carry
    lax.fori_loop(0, bt, body, 0)


def dil_sample(layer, q, kn, vn, cache_kt, cache_vt, bt=2):
    bsz, n_new, w = q.shape
    n_buf = cache_kt.shape[-1]
    tok = pl.BlockSpec((bt, n_new, w), lambda i: (i, 0, 0))
    buf = pl.BlockSpec((1, bt, DIL_HEADS, HEAD_DIM, n_buf), lambda i: (layer, i, 0, 0, 0))
    return pl.pallas_call(
        _dil_sample_kernel,
        grid=(bsz // bt,),
        in_specs=[tok, tok, tok, buf, buf],
        out_specs=tok,
        out_shape=jax.ShapeDtypeStruct((bsz, n_new, w), F32),
        compiler_params=_cparams(("parallel",)),
        name="dil_sample",
    )(q, kn, vn, cache_kt, cache_vt)


def _swa_sample_kernel(sink_ref, q_ref, kn_ref, vn_ref, kt_ref, vt_ref, o_ref):
    bt, n_new = q_ref.shape[0], q_ref.shape[1]
    n_buf = kt_ref.shape[-1]
    i_c = lax.broadcasted_iota(I32, (n_new, n_buf), 0)
    c_c = lax.broadcasted_iota(I32, (n_new, n_buf), 1)
    mask_c = (c_c >= i_c) & (n_buf + i_c - c_c <= SWA_WINDOW)
    i_n = lax.broadcasted_iota(I32, (n_new, n_new), 0)
    t_n = lax.broadcasted_iota(I32, (n_new, n_new), 1)
    mask_n = t_n <= i_n
    scale = HEAD_DIM ** -0.5
    nt = (((1,), (1,)), ((), ()))
    grp = SWA_HEADS // SWA_KV_HEADS

    def body(b, carry):
        q, kn, vn = q_ref[b], kn_ref[b], vn_ref[b]
        for pos, h in enumerate(SWA_HEAD_ORDER):
            g = h // grp
            sl = slice(pos * HEAD_DIM, (pos + 1) * HEAD_DIM)
            gl = slice(g * HEAD_DIM, (g + 1) * HEAD_DIM)
            qh = q[:, sl].astype(BF16)
            kt = kt_ref[0, b, g].astype(BF16)
            vt = vt_ref[0, b, g].astype(BF16)
            s_c = jnp.where(mask_c, jnp.dot(qh, kt, preferred_element_type=F32) * scale, -jnp.inf)
            s_n = jnp.where(mask_n, lax.dot_general(qh, kn[:, gl].astype(BF16), nt, preferred_element_type=F32) * scale,
                            -jnp.inf)
            sink = sink_ref[h]
            m = jnp.maximum(jnp.maximum(s_c.max(-1, keepdims=True), s_n.max(-1, keepdims=True)), sink)
            e_c = jnp.exp(s_c - m)
            e_n = jnp.exp(s_n - m)
            den = e_c.sum(-1, keepdims=True) + e_n.sum(-1, keepdims=True) + jnp.exp(sink - m)
            o_ref[b, :, sl] = (lax.dot_general((e_c / den).astype(BF16), vt, nt, preferred_element_type=F32)
                               + jnp.dot((e_n / den).astype(BF16), vn[:, gl].astype(BF16), preferred_element_type=F32))
        return carry
    lax.fori_loop(0, bt, body, 0)


def swa_sample(layer, sinks, q, kn, vn, cache_kt, cache_vt, bt=16):
    bsz, n_new, w = q.shape
    n_buf = cache_kt.shape[-1]
    tok = lambda n: pl.BlockSpec((bt, n_new, n), lambda i: (i, 0, 0))
    buf = pl.BlockSpec((1, bt, SWA_KV_HEADS, HEAD_DIM, n_buf), lambda i: (layer, i, 0, 0, 0))
    kv = SWA_KV_HEADS * HEAD_DIM
    return pl.pallas_call(
        _swa_sample_kernel,
        grid=(bsz // bt,),
        in_specs=[pl.BlockSpec(memory_space=pltpu.SMEM), tok(w), tok(kv), tok(kv), buf, buf],
        out_specs=tok(w),
        out_shape=jax.ShapeDtypeStruct((bsz, n_new, w), F32),
        compiler_params=_cparams(("parallel",)),
        name="swa_sample",
    )(sinks, q, kn, vn, cache_kt, cache_vt)


def _mlstm_sample_kernel(q_ref, k_ref, v_ref, kc_ref, om_ref, gc_ref, gr_ref, brow_ref, bcol_ref, nw_ref,
                         c_in, n_in, m_in, o_ref, c_out, n_out, m_out):
    bt, L = q_ref.shape[0], q_ref.shape[1]
    H = MLSTM_HEADS
    r = lax.broadcasted_iota(I32, (L, L), 0)
    c = lax.broadcasted_iota(I32, (L, L), 1)
    causal = c <= r

    def body(b, carry):
        g_col = gc_ref[b] + brow_ref[...]
        g_row = gr_ref[b] + bcol_ref[...]
        for h in range(H):
            sl = slice(h * MLSTM_DK, (h + 1) * MLSTM_DK)
            q, k, v = q_ref[b, :, sl], k_ref[b, :, sl], v_ref[b, :, sl]
            k_col = kc_ref[b, h]
            c_prev, n_prev, m_prev = c_in[0, b, h], n_in[0, b, h], m_in[0, b, h][:, 0:1]
            ig_col, ig_row = g_col[:, h:h + 1], g_row[h:h + 1, :]
            lf_col, lf_row = _log_sigmoid(g_col[:, H + h:H + h + 1]), _log_sigmoid(g_row[H + h:H + h + 1, :])
            b_col = jnp.where(causal, lf_row, 0.0).sum(-1, keepdims=True)
            b_row = jnp.where(r <= c, lf_col, 0.0).sum(0, keepdims=True)

            log_d = jnp.where(causal, b_col - b_row + ig_row, -jnp.inf)
            log_inter = b_col + m_prev
            m_t = jnp.maximum(log_inter, log_d.max(-1, keepdims=True))
            s_qk = lax.dot_general(q.astype(BF16), k.astype(BF16), (((1,), (1,)), ((), ())), preferred_element_type=F32)
            w_ts = jnp.exp(log_d - m_t) * s_qk
            w_inter = jnp.exp(log_inter - m_t)
            num = w_inter * jnp.dot(q.astype(BF16), c_prev.astype(BF16), preferred_element_type=F32)
            round_bf = lambda a: a.astype(BF16).astype(F32)
            w_ts_r = round_bf(w_ts)
            for s in range(L):
                num = num + w_ts_r[:, s:s + 1] * v[s:s + 1, :]
            den = w_ts.sum(-1, keepdims=True) + w_inter * (q * round_bf(n_prev)).sum(-1, keepdims=True)
            hh = num / jnp.maximum(jnp.abs(den), jnp.exp(-m_t))

            mu = hh.mean(-1, keepdims=True)
            var = jnp.mean(jnp.square(hh - mu), -1, keepdims=True)
            hn = (hh - mu) * lax.rsqrt(var + LN_EPS) * nw_ref[:, sl]
            o_ref[b, :, sl] = hn * jax.nn.sigmoid(om_ref[b, :, sl])

            b_last = b_col[L - 1:L, :]
            m_new = jnp.maximum(b_last + m_prev, (b_last - b_row + ig_row).max(-1, keepdims=True))
            w_s = jnp.exp(b_last - b_col + ig_col - m_new)
            w_s_row = jnp.exp(b_last - b_row + ig_row - m_new)
            decay = jnp.exp(b_last + m_prev - m_new)
            c_out[b, h] = decay * c_prev + jnp.dot((k_col * w_s_row).astype(BF16), v.astype(BF16), preferred_element_type=F32)
            n_out[b, h] = decay * n_prev + (round_bf(w_s) * round_bf(k)).sum(0, keepdims=True)
            m_out[b, h] = jnp.broadcast_to(m_new, (1, LANES))
        return carry
    lax.fori_loop(0, bt, body, 0, unroll=2)


def mlstm_sample(layer, q, k, v, k_col, om, gc, gr, bias_row, bias_col, norm_w, c_in, n_in, m_in, bt=8):
    bsz, L, w = q.shape
    H = MLSTM_HEADS
    lead = lambda *tail: pl.BlockSpec((bt,) + tail, lambda i: (i,) + (0,) * len(tail))
    const = lambda shape: pl.BlockSpec(shape, lambda i: (0,) * len(shape))
    st_in = lambda *tail: pl.BlockSpec((1, bt, H) + tail, lambda i: (layer, i, 0) + (0,) * len(tail))
    return pl.pallas_call(
        _mlstm_sample_kernel,
        grid=(bsz // bt,),
        in_specs=[lead(L, w), lead(L, w), lead(L, w), lead(H, MLSTM_DK, L), lead(L, w),
                  lead(L, 8), lead(8, L), const((1, 8)), const((8, 1)), const((1, w)),
                  st_in(MLSTM_DK, MLSTM_DV), st_in(1, MLSTM_DK), st_in(1, LANES)],
        out_specs=[lead(L, w), lead(H, MLSTM_DK, MLSTM_DV), lead(H, 1, MLSTM_DK), lead(H, 1, LANES)],
        out_shape=[jax.ShapeDtypeStruct((bsz, L, w), F32),
                   jax.ShapeDtypeStruct((bsz, H, MLSTM_DK, MLSTM_DV), F32),
                   jax.ShapeDtypeStruct((bsz, H, 1, MLSTM_DK), F32),
                   jax.ShapeDtypeStruct((bsz, H, 1, LANES), F32)],
        compiler_params=_cparams(("parallel",)),
        name="mlstm_sample",
    )(q, k, v, k_col, om, gc, gr, bias_row, bias_col, norm_w, c_in, n_in, m_in)


def _cache_roll_kernel(k_ref, v_ref, kn_ref, vn_ref, ko_ref, vo_ref):
    shape = k_ref.shape
    n_buf, n_new = shape[-1], kn_ref.shape[-1]
    rows = shape[1] * shape[2] * shape[3]
    lane = lax.broadcasted_iota(I32, (rows, LANES), 1)
    for src, new, dst in ((k_ref, kn_ref, ko_ref), (v_ref, vn_ref, vo_ref)):
        rolled = pltpu.roll(src[0].reshape(rows, n_buf), n_buf - n_new, 1)
        tail = rolled[:, n_buf - LANES:]
        fresh = new[0].reshape(rows, n_new)
        for t in range(n_new):
            tail = jnp.where(lane == LANES - n_new + t, fresh[:, t:t + 1], tail)
        dst[0] = rolled.reshape(shape[1:])
        dst[0, :, :, :, n_buf - LANES:] = tail.reshape(shape[1:4] + (LANES,))


def cache_roll(cache_kt, cache_vt, new_kt, new_vt, bt):
    depth, bsz, h, d, n_buf = cache_kt.shape
    n_new = new_kt.shape[-1]
    blk = lambda n: pl.BlockSpec((1, bt, h, d, n), lambda l, i: (l, i, 0, 0, 0))
    return pl.pallas_call(
        _cache_roll_kernel,
        grid=(depth, bsz // bt),
        in_specs=[blk(n_buf), blk(n_buf), blk(n_new), blk(n_new)],
        out_specs=[blk(n_buf), blk(n_buf)],
        out_shape=[jax.ShapeDtypeStruct(cache_kt.shape, F32)] * 2,
        compiler_params=_cparams(("parallel", "parallel")),
        name="cache_roll",
    )(cache_kt, cache_vt, new_kt, new_vt)


PAST_LEN = 8192


def kernel(x_prompt, x_sample, cache_swa_k, cache_swa_v, cache_dil_k, cache_dil_v, state_mlstm_c, state_mlstm_n, state_mlstm_m, w_in, w_out, attn_sinks, mlstm_b_i, mlstm_b_f, mlstm_norm_w, ln1_g, ln1_b, ln2_g, ln2_b, router_w, router_b, moe_w_gate, moe_w_up, moe_w_down):
    bsz, t_len, d = x_prompt.shape
    sbsz, s_len, _ = x_sample.shape
    n_p, n_s = bsz * t_len, sbsz * s_len
    n_tok = n_p + n_s
    depth = w_in.shape[0]

    x = jnp.concatenate([x_prompt.reshape(n_p, d), x_sample.reshape(n_s, d)], axis=0)
    pos = jnp.concatenate([jnp.tile(jnp.arange(t_len, dtype=I32), bsz),
                           jnp.tile(PAST_LEN + jnp.arange(s_len, dtype=I32), sbsz)])
    cos, sa, sb = rope_tables(pos)

    to_lanes = lambda c: jnp.transpose(c, (0, 1, 3, 4, 2))
    from_lanes = lambda c: jnp.transpose(c, (0, 1, 4, 2, 3))
    swa_kt, swa_vt, dil_kt, dil_vt = map(to_lanes, (cache_swa_k, cache_swa_v, cache_dil_k, cache_dil_v))
    n_in = state_mlstm_n[:, :, :, None, :]
    m_in = jnp.broadcast_to(state_mlstm_m[..., None, None], state_mlstm_m.shape + (1, LANES))
    rw_t, rb_col = router_w.T, router_b[:, None]

    order = jnp.array(SWA_HEAD_ORDER)
    n_qa = SWA_HEADS * HEAD_DIM
    w_in_b = w_in.astype(BF16)
    w_qa = w_in_b[:, :, :n_qa].reshape(depth, d, SWA_HEADS, HEAD_DIM)[:, :, order].reshape(depth, d, n_qa)
    w_main = jnp.concatenate([w_qa, w_in_b[:, :, n_qa:_MAIN_COLS]], axis=2)
    w_gates = w_in_b[:, :, _MAIN_COLS:]
    w_gates_t = jnp.transpose(w_gates, (0, 2, 1))
    w_out_b = w_out.astype(BF16)
    w_oa = w_out_b[:, :n_qa].reshape(depth, SWA_HEADS, HEAD_DIM, d)[:, order].reshape(depth, n_qa, d)
    w_out_b = jnp.concatenate([w_oa, w_out_b[:, n_qa:]], axis=1)

    p_states, s_states, new_cols = [], [], []
    for l in range(depth):
        qa, ka, va, qd, kd, vd, qm, km, vm, om, gc, gr = in_proj(x, w_main[l], w_gates[l], w_gates_t[l], cos, sa, sb)
        bias = jnp.concatenate([mlstm_b_i[l], mlstm_b_f[l]])
        bias_row, bias_col = bias[None, :], bias[:, None]
        norm_w = mlstm_norm_w[l][None, :]

        oa_p = swa_prompt(attn_sinks[l], qa, ka, va, bsz, t_len)
        od_p = dil_prompt(qd, kd, vd, bsz, t_len)
        om_p, pc, pn, pm = mlstm_prompt(qm, km, vm, om, gc, gr, bias_row, bias_col, norm_w, bsz, t_len)
        om_p = om_p.reshape(n_p, om_p.shape[2])

        s3 = lambda a: a[n_p:].astype(F32).reshape(sbsz, s_len, a.shape[1])
        qa_s, ka_s, va_s, qd_s, kd_s, vd_s, qm_s, km_s, vm_s, om_s, gc_s = map(
            s3, (qa, ka, va, qd, kd, vd, qm, km, vm, om, gc))
        gr_s = jnp.transpose(gc_s, (0, 2, 1))
        col = lambda a: jnp.transpose(a.reshape(sbsz, s_len, MLSTM_HEADS, MLSTM_DK), (0, 2, 3, 1))
        oa_s = swa_sample(l, attn_sinks[l], qa_s, ka_s, va_s, swa_kt, swa_vt)
        od_s = dil_sample(l, qd_s, kd_s, vd_s, dil_kt, dil_vt)
        omx_s, sc, sn, sm = mlstm_sample(l, qm_s, km_s, vm_s, col(km_s), om_s, gc_s, gr_s,
                                         bias_row, bias_col, norm_w, state_mlstm_c, n_in, m_in)
        flat = lambda a: a.reshape(n_s, a.shape[2]).astype(BF16)

        x1t, e_ids, gates = out_proj(oa_p, od_p, om_p, flat(oa_s), flat(od_s), flat(omx_s), x,
                                     w_out_b[l], ln1_g[l][None, :], ln1_b[l][None, :], rw_t, rb_col)
        y = moe_experts(l, x1t, *moe_plan(e_ids, n_tok), moe_w_gate, moe_w_up, moe_w_down)
        x = moe_combine_ln2(x1t, y, gates.T, ln2_g[l][None, :], ln2_b[l][None, :])

        tail = lambda a, h, keep: jnp.stack([a[(b + 1) * t_len - keep:(b + 1) * t_len] for b in range(bsz)]
                                            ).reshape(bsz, keep, h, HEAD_DIM)
        swa_keep, dil_keep = min(SWA_WINDOW, t_len), min(DIL_WINDOW_MAX, t_len)
        p_states.append((tail(ka, SWA_KV_HEADS, swa_keep), tail(va, SWA_KV_HEADS, swa_keep),
                         tail(kd, DIL_HEADS, dil_keep), tail(vd, DIL_HEADS, dil_keep),
                         pc, pn[:, :, 0, :], pm[:, :, 0, 0]))
        s_states.append((sc, sn[:, :, 0, :], sm[:, :, 0, 0]))
        new_t = lambda a, h: jnp.transpose(a.reshape(sbsz, s_len, h, HEAD_DIM), (0, 2, 3, 1))
        new_cols.append((new_t(ka_s, SWA_KV_HEADS), new_t(va_s, SWA_KV_HEADS), new_t(kd_s, DIL_HEADS), new_t(vd_s, DIL_HEADS)))

    p_swa_k, p_swa_v, p_dil_k, p_dil_v, p_c, p_n, p_m = [jnp.stack(a) for a in zip(*p_states)]
    s_c, s_n, s_m = [jnp.stack(a) for a in zip(*s_states)]
    nk_swa, nv_swa, nk_dil, nv_dil = [jnp.stack(a) for a in zip(*new_cols)]
    s_swa_kt, s_swa_vt = cache_roll(swa_kt, swa_vt, nk_swa, nv_swa, bt=16)
    s_dil_kt, s_dil_vt = cache_roll(dil_kt, dil_vt, nk_dil, nv_dil, bt=1)
    return (x[:n_p].reshape(bsz, t_len, d), x[n_p:].reshape(sbsz, s_len, d),
            p_swa_k, p_swa_v, p_dil_k, p_dil_v, p_c, p_n, p_m,
            from_lanes(s_swa_kt), from_lanes(s_swa_vt), from_lanes(s_dil_kt), from_lanes(s_dil_vt), s_c, s_n, s_m)
```

```python
import functools

import jax
import jax.numpy as jnp
from jax import lax
from jax.experimental import pallas as pl
from jax.experimental.pallas import tpu as pltpu

F32 = jnp.float32
BF16 = jnp.bfloat16
I32 = jnp.int32

D_MODEL = 1024
DEPTH = 4
HEAD_DIM = 64
SWA_HEADS = 4
SWA_KV_HEADS = 2
SWA_WINDOW = 128
DIL_HEADS = 4
DIL_PAIRS = ((128, 1), (512, 4), (2048, 16))
DIL_WINDOW_MAX = 2048
MLSTM_HEADS = 4
MLSTM_DK = 128
MLSTM_DV = 128
ROPE_DIM = HEAD_DIM // 4
ROPE_THETA = 500000.0
N_EXPERTS = 16
N_EXPERT_GROUPS = 4
EXPERTS_PER_GROUP = 4
TOP_K = 2
D_EXPERT = 512
LN_EPS = 1e-5
DEEPNORM_ALPHA = (2.0 * DEPTH) ** 0.25

_QA, _KA, _VA, _QD, _KD, _VD, _QM, _KM, _VM, _OM, _GATES = 0, 256, 384, 512, 768, 1024, 1280, 1792, 2304, 2816, 3328
_MAIN_COLS = 3328

LANES = 128
SUBLANES = 8
VMEM_LIMIT = 56 * 1024 * 1024


def _cparams(sem):
    return pltpu.CompilerParams(dimension_semantics=sem, vmem_limit_bytes=VMEM_LIMIT)


def _in_proj_kernel(x_ref, w_ref, wg_ref, wgt_ref, cos_ref, sa_ref, sb_ref,
                    qa_ref, ka_ref, va_ref, qd_ref, kd_ref, vd_ref,
                    qm_ref, km_ref, vm_ref, om_ref, gc_ref, gr_ref):
    xb = x_ref[...].astype(BF16)
    cos = cos_ref[...]
    sa = sa_ref[...]
    sb = sb_ref[...]

    def seg(c0, n):
        return jnp.dot(xb, w_ref[:, c0:c0 + n], preferred_element_type=F32)

    def rope_store(out_ref, c0, n):
        for j in range(n // LANES):
            v = seg(c0 + j * LANES, LANES)
            r = v * cos + pltpu.roll(v, LANES - ROPE_DIM // 2, 1) * sa + pltpu.roll(v, ROPE_DIM // 2, 1) * sb
            out_ref[:, j * LANES:(j + 1) * LANES] = r.astype(out_ref.dtype)

    rope_store(qa_ref, _QA, 256)
    rope_store(ka_ref, _KA, 128)
    va_ref[...] = seg(_VA, 128)
    rope_store(qd_ref, _QD, 256)
    rope_store(kd_ref, _KD, 256)
    vd_ref[...] = seg(_VD, 256)
    qm_ref[...] = (seg(_QM, 512) * (MLSTM_DK ** -0.5)).astype(BF16)
    km_ref[...] = seg(_KM, 512)
    vm_ref[...] = seg(_VM, 512).astype(BF16)
    om_ref[...] = seg(_OM, 512)
    gc_ref[...] = jnp.dot(xb, wg_ref[...], preferred_element_type=F32)
    gr_ref[...] = lax.dot_general(wgt_ref[...], xb, (((1,), (1,)), ((), ())), preferred_element_type=F32)


def in_proj(x, w_main, w_g, w_gt, cos, sa, sb, tm=512):
    nt = x.shape[0]
    row = lambda n: pl.BlockSpec((tm, n), lambda i: (i, 0))
    full = lambda a: pl.BlockSpec(a.shape, lambda i: (0, 0))
    outs = [(256, BF16), (128, F32), (128, F32), (256, F32), (256, F32), (256, F32),
            (512, BF16), (512, F32), (512, BF16), (512, F32), (8, F32)]
    out_shape = [jax.ShapeDtypeStruct((nt, n), dt) for n, dt in outs] + [jax.ShapeDtypeStruct((8, nt), F32)]
    out_specs = [row(n) for n, _ in outs] + [pl.BlockSpec((8, tm), lambda i: (0, i))]
    return pl.pallas_call(
        _in_proj_kernel,
        grid=(nt // tm,),
        in_specs=[row(D_MODEL), full(w_main), full(w_g), full(w_gt), row(LANES), row(LANES), row(LANES)],
        out_specs=out_specs,
        out_shape=out_shape,
        compiler_params=_cparams(("parallel",)),
        name="in_proj",
    )(x, w_main, w_g, w_gt, cos, sa, sb)


def rope_tables(pos):
    half = ROPE_DIM // 2
    inv_freq = ROPE_THETA ** (-jnp.arange(half, dtype=F32) / half)
    ang = pos.astype(F32)[:, None] * inv_freq[None, :]
    c, s = jnp.cos(ang), jnp.sin(ang)
    n = pos.shape[0]
    one = jnp.ones((n, HEAD_DIM - ROPE_DIM), F32)
    zero = jnp.zeros((n, HEAD_DIM - ROPE_DIM), F32)
    zh = jnp.zeros((n, half), F32)
    cos64 = jnp.concatenate([c, c, one], axis=1)
    sa64 = jnp.concatenate([-s, zh, zero], axis=1)
    sb64 = jnp.concatenate([zh, s, zero], axis=1)
    tile2 = lambda a: jnp.concatenate([a, a], axis=1)
    return tile2(cos64), tile2(sa64), tile2(sb64)


SWA_HEAD_ORDER = (0, 3, 1, 2)
SWA_BLOCK = 128
SWA_STEP = 512


def _swa_prompt_kernel(sink_ref, q_ref, kp_ref, kc_ref, vp_ref, vc_ref, o_ref):
    i = pl.program_id(1)
    n = SWA_BLOCK
    r = lax.broadcasted_iota(I32, (n, 2 * n), 0)
    c = lax.broadcasted_iota(I32, (n, 2 * n), 1)
    band = ((c < n) & (c >= r)) | ((c >= n) & (c - n <= r))
    lane = lax.broadcasted_iota(I32, (n, LANES), 1)
    low = lane < HEAD_DIM
    scale = HEAD_DIM ** -0.5
    nt = (((1,), (1,)), ((), ()))
    for j in range(SWA_STEP // n):
        rows = slice(j * n, (j + 1) * n)
        if j == 0:
            kk = jnp.concatenate([kp_ref[...], kc_ref[rows, :]], axis=0)
            vv = jnp.concatenate([vp_ref[...], vc_ref[rows, :]], axis=0)
            mask = band & ((c >= n) | (i > 0))
        else:
            kk = kc_ref[(j - 1) * n:(j + 1) * n, :]
            vv = vc_ref[(j - 1) * n:(j + 1) * n, :]
            mask = band
        kb, vb = kk.astype(BF16), vv.astype(BF16)
        for col in range(SWA_HEADS * HEAD_DIM // LANES):
            q = q_ref[rows, col * LANES:(col + 1) * LANES]
            halves = []
            for half in range(LANES // HEAD_DIM):
                own = low if half == 0 else ~low
                qh = jnp.where(own, q, jnp.zeros_like(q))
                s = jnp.where(mask, lax.dot_general(qh, kb, nt, preferred_element_type=F32) * scale, -jnp.inf)
                sink = sink_ref[SWA_HEAD_ORDER[col * (LANES // HEAD_DIM) + half]]
                m = jnp.maximum(s.max(-1, keepdims=True), sink)
                p = jnp.exp(s - m)
                p = p / (p.sum(-1, keepdims=True) + jnp.exp(sink - m))
                halves.append(jnp.dot(p.astype(BF16), vb, preferred_element_type=F32))
            o_ref[rows, col * LANES:(col + 1) * LANES] = jnp.where(low, halves[0], halves[1]).astype(o_ref.dtype)


def swa_prompt(sinks, qa, ka, va, bsz, t_len):
    nb = t_len // SWA_STEP
    per = SWA_STEP // SWA_BLOCK
    cur = lambda b, i: (b * nb + i, 0)
    prev = lambda b, i: (jnp.maximum((b * nb + i) * per - 1, 0), 0)
    kv = SWA_KV_HEADS * HEAD_DIM
    w = SWA_HEADS * HEAD_DIM
    return pl.pallas_call(
        _swa_prompt_kernel,
        grid=(bsz, nb),
        in_specs=[pl.BlockSpec(memory_space=pltpu.SMEM),
                  pl.BlockSpec((SWA_STEP, w), cur),
                  pl.BlockSpec((SWA_BLOCK, kv), prev), pl.BlockSpec((SWA_STEP, kv), cur),
                  pl.BlockSpec((SWA_BLOCK, kv), prev), pl.BlockSpec((SWA_STEP, kv), cur)],
        out_specs=pl.BlockSpec((SWA_STEP, w), cur),
        out_shape=jax.ShapeDtypeStruct((bsz * t_len, w), BF16),
        compiler_params=_cparams(("parallel", "parallel")),
        name="swa_prompt",
    )(sinks, qa, ka, ka, va, va)


DIL_TILE = 2048
DIL_BLOCK = 128


def _split_bf16(a):
    hi = a.astype(BF16)
    return hi, (a - hi.astype(F32)).astype(BF16)


def _dot_f32x3(a, b, dims):
    (a_hi, a_lo), (b_hi, b_lo) = (a if isinstance(a, tuple) else _split_bf16(a)), (b if isinstance(b, tuple) else _split_bf16(b))
    dot = lambda x, y: lax.dot_general(x, y, dims, preferred_element_type=F32)
    return (dot(a_lo, b_hi) + dot(a_hi, b_lo)) + dot(a_hi, b_hi)


_NT = (((1,), (1,)), ((), ()))
_NN = (((1,), (0,)), ((), ()))


def _dil_scores(qv, kk, prev_valid):
    n = DIL_BLOCK
    r = lax.broadcasted_iota(I32, (n, 2 * n), 0)
    c = lax.broadcasted_iota(I32, (n, 2 * n), 1)
    mask = ((c < n) & (c >= r) & prev_valid) | ((c >= n) & (c - n <= r))
    lane = lax.broadcasted_iota(I32, (n, LANES), 1)
    scale = HEAD_DIM ** -0.5
    out = []
    kk_split = _split_bf16(kk)
    for h in range(LANES // HEAD_DIM):
        own = (lane >= h * HEAD_DIM) & (lane < (h + 1) * HEAD_DIM)
        qh = jnp.where(own, qv, 0.0)
        s = _dot_f32x3(qh, kk_split, _NT) * scale
        out.append((own, jnp.where(mask, s, -jnp.inf)))
    return out


def _pick_head(parts):
    return jnp.where(parts[0][0], parts[0][1], parts[1][1])


def _dil_prompt_kernel(q_ref, kp_ref, kc_ref, vp_ref, vc_ref, o_ref, *scratch):
    has_prev_tile = pl.program_id(1) > 0
    n = DIL_BLOCK
    n_pat = len(DIL_PAIRS)
    a_refs, m_refs, l_refs = scratch[:n_pat], scratch[n_pat:2 * n_pat], scratch[2 * n_pat:3 * n_pat]
    max_ref = scratch[3 * n_pat]

    def sweep(unit):
        for pi, (_, d) in enumerate(DIL_PAIRS):
            span = n * d
            for res in range(d):
                rows0 = pl.ds(res, n, stride=d) if d > 1 else pl.ds(0, n)
                prow = pl.ds(DIL_TILE - span + res, n, stride=d) if d > 1 else pl.ds(DIL_TILE - n, n)

                def get0(ref, rows0=rows0):
                    return ref[rows0, :]

                def keys0(prev_ref, cur_ref, rows0=rows0, prow=prow):
                    return jnp.concatenate([prev_ref[prow, :], cur_ref[rows0, :]], axis=0)

                def put0(ref, val, rows0=rows0):
                    ref[rows0, :] = val
                unit(pi, get0, keys0, put0, has_prev_tile)

                n_blk = DIL_TILE // span
                if n_blk > 1:
                    def body(j, carry, pi=pi, d=d, res=res, span=span):
                        win = pl.ds(pl.multiple_of((j - 1) * span, span), 2 * span)
                        both = pl.ds(res, 2 * n, stride=d) if d > 1 else pl.ds(0, 2 * n)
                        crow = pl.ds(span + res, n, stride=d) if d > 1 else pl.ds(n, n)

                        def get(ref):
                            return ref.at[win, :][crow, :]

                        def keys(prev_ref, cur_ref):
                            return cur_ref.at[win, :][both, :]

                        def put(ref, val):
                            ref.at[win, :][crow, :] = val
                        unit(pi, get, keys, put, True)
                        return carry
                    lax.fori_loop(1, n_blk, body, 0, unroll=3 if (n_blk - 1) % 3 == 0 else 1)

    def find_max(pi, get, keys, put, prev_valid):
        parts = _dil_scores(get(q_ref), keys(kp_ref, kc_ref), prev_valid)
        put(m_refs[pi], _pick_head([(own, s.max(-1, keepdims=True)) for own, s in parts]))

    sweep(find_max)
    max_ref[...] = functools.reduce(jnp.maximum, [m[...] for m in m_refs])

    def accumulate(pi, get, keys, put, prev_valid):
        parts = _dil_scores(get(q_ref), keys(kp_ref, kc_ref), prev_valid)
        vv_split = _split_bf16(keys(vp_ref, vc_ref))
        m_all = get(max_ref)
        acc, den = [], []
        for h, (own, s) in enumerate(parts):
            e = jnp.exp(s - m_all[:, h * HEAD_DIM:h * HEAD_DIM + 1])
            den.append((own, e.sum(-1, keepdims=True)))
            acc.append((own, _dot_f32x3(e, vv_split, _NN)))
        put(a_refs[pi], _pick_head(acc))
        put(l_refs[pi], _pick_head(den))

    sweep(accumulate)

    chunk = 256

    def merge(i, carry):
        rows = pl.ds(pl.multiple_of(i * chunk, chunk), chunk)
        dens = [l[rows, :] for l in l_refs]
        num = functools.reduce(jnp.add, [den * (a[rows, :] / den) for a, den in zip(a_refs, dens)])
        o_ref[rows, :] = (num / functools.reduce(jnp.add, dens)).astype(o_ref.dtype)
        return carry
    lax.fori_loop(0, DIL_TILE // chunk, merge, 0)


def dil_prompt(qd, kd, vd, bsz, t_len):
    ntile = t_len // DIL_TILE
    cur = lambda b, i, p: (b * ntile + i, p)
    prev = lambda b, i, p: (b * ntile + jnp.maximum(i - 1, 0), p)
    w = DIL_HEADS * HEAD_DIM
    blk = lambda im: pl.BlockSpec((DIL_TILE, LANES), im)
    return pl.pallas_call(
        _dil_prompt_kernel,
        grid=(bsz, ntile, w // LANES),
        in_specs=[blk(cur), blk(prev), blk(cur), blk(prev), blk(cur)],
        out_specs=blk(cur),
        out_shape=jax.ShapeDtypeStruct((bsz * t_len, w), BF16),
        scratch_shapes=[pltpu.VMEM((DIL_TILE, LANES), F32)] * (3 * len(DIL_PAIRS) + 1),
        compiler_params=_cparams(("parallel", "parallel", "parallel")),
        name="dil_prompt",
    )(qd, kd, kd, vd, vd)


MLSTM_CHUNK = 64
MLSTM_STEP = 128
_HI = lax.Precision.HIGHEST


def _log_sigmoid(x):
    return jnp.minimum(x, 0.0) - jnp.log1p(jnp.exp(-jnp.abs(x)))


def _mlstm_prompt_kernel(n_seq, *refs):
    ins, (brow_ref, bcol_ref, nw_ref) = refs[:6 * n_seq], refs[6 * n_seq:6 * n_seq + 3]
    o_ref, c_ref, n_ref, m_ref = refs[6 * n_seq + 3:]
    q_refs, k_refs, v_refs, om_refs, gc_refs, gr_refs = [ins[i * n_seq:(i + 1) * n_seq] for i in range(6)]
    L = MLSTM_CHUNK

    @pl.when(pl.program_id(0) == 0)
    def _():
        c_ref[...] = jnp.zeros_like(c_ref)
        n_ref[...] = jnp.zeros_like(n_ref)
        m_ref[...] = jnp.zeros_like(m_ref)

    r = lax.broadcasted_iota(I32, (L, L), 0)
    c = lax.broadcasted_iota(I32, (L, L), 1)
    causal = c <= r
    tril = causal.astype(F32)
    triu = (r <= c).astype(F32)
    nt = (((1,), (1,)), ((), ()))
    H = MLSTM_HEADS
    for sub in range(MLSTM_STEP // L):
        rows = slice(sub * L, (sub + 1) * L)
        for b in range(n_seq):
            g_col = gc_refs[b][rows, :] + brow_ref[...]
            g_row = gr_refs[b][:, rows] + bcol_ref[...]
            bcum_col = jnp.dot(tril, _log_sigmoid(g_col), precision=_HI, preferred_element_type=F32)
            bcum_row = jnp.dot(_log_sigmoid(g_row), triu, precision=_HI, preferred_element_type=F32)
            for h in range(H):
                sl = slice(h * MLSTM_DK, (h + 1) * MLSTM_DK)
                ig_col, ig_row = g_col[:, h:h + 1], g_row[h:h + 1, :]
                b_col, b_row = bcum_col[:, H + h:H + h + 1], bcum_row[H + h:H + h + 1, :]
                m_prev = m_ref[b, h, :, 0:1]
                c_prev = c_ref[b, h]
                n_prev = n_ref[b, h]
                q, kf, v = q_refs[b][rows, sl], k_refs[b][rows, sl], v_refs[b][rows, sl]
                k = kf.astype(BF16)
                n_b = n_prev.astype(BF16).astype(F32)

                log_d = jnp.where(causal, b_col - b_row + ig_row, -jnp.inf)
                log_inter = b_col + m_prev
                m_t = jnp.maximum(log_inter, log_d.max(-1, keepdims=True))
                w_ts = jnp.exp(log_d - m_t) * lax.dot_general(q, k, nt, preferred_element_type=F32)
                w_inter = jnp.exp(log_inter - m_t)
                num = (jnp.dot(w_ts.astype(BF16), v, preferred_element_type=F32)
                       + w_inter * jnp.dot(q, c_prev.astype(BF16), preferred_element_type=F32))
                den = w_ts.sum(-1, keepdims=True) + w_inter * (q.astype(F32) * n_b).sum(-1, keepdims=True)
                hh = num / jnp.maximum(jnp.abs(den), jnp.exp(-m_t))

                mu = hh.mean(-1, keepdims=True)
                var = jnp.mean(jnp.square(hh - mu), -1, keepdims=True)
                hn = (hh - mu) * lax.rsqrt(var + LN_EPS) * nw_ref[:, sl]
                o_ref[b, rows, sl] = (hn * jax.nn.sigmoid(om_refs[b][rows, sl])).astype(o_ref.dtype)

                b_last = b_col[L - 1:L, :]
                m_new = jnp.maximum(b_last + m_prev, (b_last - b_row + ig_row).max(-1, keepdims=True))
                w_s = jnp.exp(b_last - b_col + ig_col - m_new)
                decay = jnp.exp(b_last + m_prev - m_new)
                c_ref[b, h] = decay * c_prev + jnp.dot((w_s * kf).T.astype(BF16), v, preferred_element_type=F32)
                n_ref[b, h] = decay * n_prev + (w_s.astype(BF16).astype(F32) * k.astype(F32)).sum(0, keepdims=True)
                m_ref[b, h] = jnp.broadcast_to(m_new, (1, LANES))


def mlstm_prompt(qm, km, vm, om, gc, gr, bias_row, bias_col, norm_w, bsz, t_len):
    S = MLSTM_STEP
    ns = t_len // S
    H = MLSTM_HEADS
    w = H * MLSTM_DV
    rows = lambda n: [pl.BlockSpec((S, n), lambda j, b=b: (b * ns + j, 0)) for b in range(bsz)]
    cols = [pl.BlockSpec((8, S), lambda j, b=b: (0, b * ns + j)) for b in range(bsz)]
    const = lambda shape: pl.BlockSpec(shape, lambda j: (0,) * len(shape))
    st = lambda *tail: pl.BlockSpec((bsz, H) + tail, lambda j: (0,) * (2 + len(tail)))
    args = [qm] * bsz + [km] * bsz + [vm] * bsz + [om] * bsz + [gc] * bsz + [gr] * bsz
    return pl.pallas_call(
        functools.partial(_mlstm_prompt_kernel, bsz),
        grid=(ns,),
        in_specs=rows(w) + rows(w) + rows(w) + rows(w) + rows(8) + cols + [const((1, 8)), const((8, 1)), const((1, w))],
        out_specs=[pl.BlockSpec((bsz, S, w), lambda j: (0, j, 0)), st(MLSTM_DK, MLSTM_DV), st(1, MLSTM_DK), st(1, LANES)],
        out_shape=[jax.ShapeDtypeStruct((bsz, t_len, w), BF16),
                   jax.ShapeDtypeStruct((bsz, H, MLSTM_DK, MLSTM_DV), F32),
                   jax.ShapeDtypeStruct((bsz, H, 1, MLSTM_DK), F32),
                   jax.ShapeDtypeStruct((bsz, H, 1, LANES), F32)],
        compiler_params=_cparams(("arbitrary",)),
        name="mlstm_prompt",
    )(*args, bias_row, bias_col, norm_w)


ROW_TILES = D_MODEL // LANES


def _store_row_tiles(ref, val):
    n = val.shape[0]
    for s in range(ROW_TILES):
        ref[pl.ds(s, n, stride=ROW_TILES), :] = val[:, s * LANES:(s + 1) * LANES]


def _load_row_tiles(ref, n):
    return jnp.concatenate([ref[pl.ds(s, n, stride=ROW_TILES), :] for s in range(ROW_TILES)], axis=1)


def _layer_norm(y, g, b):
    mu = y.mean(-1, keepdims=True)
    var = jnp.mean(jnp.square(y - mu), -1, keepdims=True)
    return (y - mu) * lax.rsqrt(var + LN_EPS) * g + b


def _group_partner(a, j, sub):
    up = pltpu.roll(a, N_EXPERTS - j, 0)
    down = pltpu.roll(a, EXPERTS_PER_GROUP - j, 0)
    return jnp.where(sub + j < EXPERTS_PER_GROUP, up, down)


def _route(logits_t):
    n = logits_t.shape[1]
    e_id = lax.broadcasted_iota(I32, (N_EXPERTS, n), 0)
    sub = e_id % EXPERTS_PER_GROUP
    grp = e_id // EXPERTS_PER_GROUP
    ex = jnp.exp(logits_t - logits_t.max(0, keepdims=True))
    aff = ex / ex.sum(0, keepdims=True)
    rank = jnp.zeros((N_EXPERTS, n), I32)
    for j in range(1, EXPERTS_PER_GROUP):
        other = _group_partner(aff, j, sub)
        other_sub = (sub + j) % EXPERTS_PER_GROUP
        rank += ((other > aff) | ((other == aff) & (other_sub < sub))).astype(I32)
    top = jnp.where(rank < TOP_K, aff, 0.0)
    score = top
    for j in range(1, EXPERTS_PER_GROUP):
        score = score + _group_partner(top, j, sub)
    grank = jnp.zeros((N_EXPERTS, n), I32)
    for j in range(1, N_EXPERT_GROUPS):
        other = pltpu.roll(score, N_EXPERTS - EXPERTS_PER_GROUP * j, 0)
        other_grp = (grp + j) % N_EXPERT_GROUPS
        grank += ((other > score) | ((other == score) & (other_grp < grp))).astype(I32)
    chosen = grank == 0
    ids, vals = [], []
    for slot in range(TOP_K):
        pick = chosen & (rank == slot)
        ids.append(jnp.where(pick, e_id, 0).sum(0, keepdims=True))
        vals.append(jnp.where(pick, aff, 0.0).sum(0, keepdims=True))
    total = vals[0] + vals[1]
    return ids, [v / total for v in vals]


def _out_proj_kernel(n_prompt_tiles, oap_ref, odp_ref, omp_ref, oas_ref, ods_ref, oms_ref, x_ref, w_ref,
                     g_ref, b_ref, rwt_ref, rb_ref, x1_ref, e_ref, gate_ref):
    is_prompt = pl.program_id(0) < n_prompt_tiles
    pick = lambda p_ref, s_ref: jnp.where(is_prompt, p_ref[...], s_ref[...])
    o_a, o_d, o_m = pick(oap_ref, oas_ref), pick(odp_ref, ods_ref), pick(omp_ref, oms_ref)
    na, nd = o_a.shape[1], o_d.shape[1]
    mix = (jnp.dot(o_a, w_ref[0:na, :], preferred_element_type=F32)
           + jnp.dot(o_d, w_ref[na:na + nd, :], preferred_element_type=F32)
           + jnp.dot(o_m, w_ref[na + nd:, :], preferred_element_type=F32))
    x1 = _layer_norm(DEEPNORM_ALPHA * x_ref[...] + mix, g_ref[...], b_ref[...])
    _store_row_tiles(x1_ref, x1)
    logits_t = lax.dot_general(rwt_ref[...].astype(BF16), x1.astype(BF16), (((1,), (1,)), ((), ())),
                               preferred_element_type=F32) + rb_ref[...]
    ids, gates = _route(logits_t)
    for slot in range(TOP_K):
        e_ref[slot:slot + 1, :] = ids[slot]
        gate_ref[slot:slot + 1, :] = gates[slot]


def out_proj(oa_p, od_p, om_p, oa_s, od_s, om_s, x, w_out, ln_g, ln_b, rw_t, rb_col, tm=512):
    nt = x.shape[0]
    n_p = oa_p.shape[0]
    assert n_p % tm == 0 and oa_s.shape[0] == tm and nt == n_p + tm
    npt = n_p // tm
    prow = lambda n: pl.BlockSpec((tm, n), lambda i: (jnp.minimum(i, npt - 1), 0))
    srow = lambda n: pl.BlockSpec((tm, n), lambda i: (0, 0))
    row = lambda n: pl.BlockSpec((tm, n), lambda i: (i, 0))
    full = lambda a: pl.BlockSpec(a.shape, lambda i: (0, 0))
    col = pl.BlockSpec((TOP_K, tm), lambda i: (0, i))
    return pl.pallas_call(
        functools.partial(_out_proj_kernel, npt),
        grid=(nt // tm,),
        in_specs=[prow(256), prow(256), prow(512), srow(256), srow(256), srow(512), row(D_MODEL), full(w_out),
                  full(ln_g), full(ln_b), full(rw_t), full(rb_col)],
        out_specs=[pl.BlockSpec((tm * ROW_TILES, LANES), lambda i: (i, 0)), col, col],
        out_shape=[jax.ShapeDtypeStruct((nt * ROW_TILES, LANES), F32), jax.ShapeDtypeStruct((TOP_K, nt), I32),
                   jax.ShapeDtypeStruct((TOP_K, nt), F32)],
        compiler_params=_cparams(("parallel",)),
        name="out_proj",
    )(oa_p, od_p, om_p, oa_s, od_s, om_s, x, w_out, ln_g, ln_b, rw_t, rb_col)


MOE_ROWS = 256


def moe_plan(e_ids, n_tok):
    n_assign = TOP_K * n_tok
    n_blocks = -(-n_assign // MOE_ROWS) + N_EXPERTS
    flat_e = e_ids.reshape(-1)
    onehot = (flat_e[:, None] == jnp.arange(N_EXPERTS, dtype=I32)[None, :]).astype(I32)
    csum = jnp.cumsum(onehot, axis=0)
    rank = jnp.take_along_axis(csum, flat_e[:, None], axis=1)[:, 0] - 1
    counts = csum[-1]
    padded = (counts + MOE_ROWS - 1) // MOE_ROWS * MOE_ROWS
    pad_end = jnp.cumsum(padded)
    dest = (pad_end - padded)[flat_e] + rank
    blk_e = jnp.clip(jnp.searchsorted(pad_end, jnp.arange(n_blocks, dtype=I32) * MOE_ROWS, side='right'),
                     0, N_EXPERTS - 1).astype(I32)
    n_used = (pad_end[-1] // MOE_ROWS).astype(I32).reshape(1)
    n_rows = n_blocks * MOE_ROWS
    spare = n_assign + jnp.arange(n_rows, dtype=I32) % MOE_ROWS
    row_dst = spare.at[dest].set(jnp.arange(n_assign, dtype=I32))
    row_tok = jnp.where(row_dst >= n_assign, 0, row_dst % n_tok)
    return blk_e, n_used, row_dst, row_tok


def _moe_kernel(n_tok, blk_e_ref, n_used_ref, row_dst_ref, row_tok_ref, x_hbm, wg_ref, wu_ref, wd_ref, y_hbm,
                xbuf, ybuf, wg_b, wu_b, wd_b, sem_g, sem_s):
    i = pl.program_id(0)
    R = MOE_ROWS
    n_used = n_used_ref[0]
    slot = i % 2

    tile = lambda row: pl.ds(pl.multiple_of(row * ROW_TILES, ROW_TILES), ROW_TILES)

    def start_gather(blk, s):
        def body(r, carry):
            tok = row_tok_ref[blk * R + r]
            pltpu.make_async_copy(x_hbm.at[tile(tok), :], xbuf.at[s, tile(r), :], sem_g.at[s]).start()
            return carry
        lax.fori_loop(0, R, body, 0, unroll=8)

    def start_scatter(blk, s):
        def body(r, carry):
            d = row_dst_ref[blk * R + r]
            pltpu.make_async_copy(ybuf.at[s, tile(r), :], y_hbm.at[tile(d), :], sem_s.at[s]).start()
            return carry
        lax.fori_loop(0, R, body, 0, unroll=8)

    block = lambda hbm, row0: hbm.at[pl.ds(row0 * ROW_TILES, R * ROW_TILES), :]
    wait_gather = lambda s: pltpu.make_async_copy(block(x_hbm, 0), xbuf.at[s], sem_g.at[s]).wait()
    wait_scatter = lambda s: pltpu.make_async_copy(ybuf.at[s], block(y_hbm, 0), sem_s.at[s]).wait()

    @pl.when(i == 0)
    def _():
        ybuf[1] = jnp.zeros(ybuf.shape[1:], F32)
        spare = pltpu.make_async_copy(ybuf.at[1], block(y_hbm, TOP_K * n_tok), sem_s.at[1])
        spare.start()
        spare.wait()

        @pl.when(n_used > 0)
        def _():
            start_gather(0, 0)

    @pl.when(i + 1 < n_used)
    def _():
        start_gather(i + 1, 1 - slot)

    @pl.when(i < n_used)
    def _():
        @pl.when((i == 0) | (blk_e_ref[i] != blk_e_ref[jnp.maximum(i - 1, 0)]))
        def _():
            wg_b[...] = wg_ref[0, 0].astype(BF16)
            wu_b[...] = wu_ref[0, 0].astype(BF16)
            wd_b[...] = wd_ref[0, 0].astype(BF16)

        wait_gather(slot)
        xb = _load_row_tiles(xbuf.at[slot], R).astype(BF16)
        hid = (jax.nn.silu(jnp.dot(xb, wg_b[...], preferred_element_type=F32))
               * jnp.dot(xb, wu_b[...], preferred_element_type=F32))
        _store_row_tiles(ybuf.at[slot], jnp.dot(hid.astype(BF16), wd_b[...], preferred_element_type=F32))
        start_scatter(i, slot)

        @pl.when(i > 0)
        def _():
            wait_scatter(1 - slot)

        @pl.when(i == n_used - 1)
        def _():
            wait_scatter(slot)


def moe_experts(layer, x1t, blk_e, n_used, row_dst, row_tok, w_gate, w_up, w_down):
    n_tok = x1t.shape[0] // ROW_TILES
    n_blocks = blk_e.shape[0]
    R = MOE_ROWS
    wspec = lambda shape: pl.BlockSpec((1, 1) + shape, lambda i, be, nu, rd, rt: (layer, be[i], 0, 0))
    grid_spec = pltpu.PrefetchScalarGridSpec(
        num_scalar_prefetch=4,
        grid=(n_blocks,),
        in_specs=[pl.BlockSpec(memory_space=pl.ANY),
                  wspec((D_MODEL, D_EXPERT)), wspec((D_MODEL, D_EXPERT)), wspec((D_EXPERT, D_MODEL))],
        out_specs=pl.BlockSpec(memory_space=pl.ANY),
        scratch_shapes=[pltpu.VMEM((2, R * ROW_TILES, LANES), F32), pltpu.VMEM((2, R * ROW_TILES, LANES), F32),
                        pltpu.VMEM((D_MODEL, D_EXPERT), BF16), pltpu.VMEM((D_MODEL, D_EXPERT), BF16),
                        pltpu.VMEM((D_EXPERT, D_MODEL), BF16),
                        pltpu.SemaphoreType.DMA((2,)), pltpu.SemaphoreType.DMA((2,))],
    )
    return pl.pallas_call(
        functools.partial(_moe_kernel, n_tok),
        grid_spec=grid_spec,
        out_shape=jax.ShapeDtypeStruct(((TOP_K * n_tok + R) * ROW_TILES, LANES), F32),
        compiler_params=_cparams(("arbitrary",)),
        name="moe_experts",
    )(blk_e, n_used, row_dst, row_tok, x1t, w_gate, w_up, w_down)


def _ln2_kernel(x1_ref, y0_ref, y1_ref, gate_ref, g_ref, b_ref, o_ref):
    tm = o_ref.shape[0]
    y = gate_ref[:, 0:1] * _load_row_tiles(y0_ref, tm) + gate_ref[:, 1:2] * _load_row_tiles(y1_ref, tm)
    o_ref[...] = _layer_norm(DEEPNORM_ALPHA * _load_row_tiles(x1_ref, tm) + y, g_ref[...], b_ref[...])


def moe_combine_ln2(x1t, y, gates_t, ln_g, ln_b, tm=512):
    nt = x1t.shape[0] // ROW_TILES
    nb = nt // tm
    row = lambda off: pl.BlockSpec((tm * ROW_TILES, LANES), lambda i: (i + off, 0))
    full = lambda a: pl.BlockSpec(a.shape, lambda i: (0, 0))
    return pl.pallas_call(
        _ln2_kernel,
        grid=(nb,),
        in_specs=[row(0), row(0), row(nb), pl.BlockSpec((tm, TOP_K), lambda i: (i, 0)), full(ln_g), full(ln_b)],
        out_specs=pl.BlockSpec((tm, D_MODEL), lambda i: (i, 0)),
        out_shape=jax.ShapeDtypeStruct((nt, D_MODEL), F32),
        compiler_params=_cparams(("parallel",)),
        name="moe_combine_ln2",
    )(x1t, y, y, gates_t, ln_g, ln_b)


def _dil_counts(n_new, n_buf, col0, n_col):
    i = lax.broadcasted_iota(I32, (n_new, n_col), 0)
    c = lax.broadcasted_iota(I32, (n_new, n_col), 1) + col0
    delta = n_buf + i - c
    cnt = jnp.zeros((n_new, n_col), F32)
    for window, dil in DIL_PAIRS:
        cnt += ((delta >= 0) & (delta <= window) & (delta % dil == 0)).astype(F32)
    return cnt


def _dil_sample_kernel(q_ref, kn_ref, vn_ref, kt_ref, vt_ref, o_ref):
    bt, n_new = q_ref.shape[0], q_ref.shape[1]
    n_buf = kt_ref.shape[-1]
    cnt_c = _dil_counts(n_new, n_buf, 0, n_buf)
    cnt_n = _dil_counts(n_new, n_buf, n_buf, n_new)
    scale = HEAD_DIM ** -0.5
    nt = (((1,), (1,)), ((), ()))

    def body(b, carry):
        q, kn, vn = q_ref[b], kn_ref[b], vn_ref[b]
        for h in range(DIL_HEADS):
            sl = slice(h * HEAD_DIM, (h + 1) * HEAD_DIM)
            qh = q[:, sl]
            qh_split = _split_bf16(qh)
            s_c = _dot_f32x3(qh_split, kt_ref[0, b, h], _NN) * scale
            s_n = _dot_f32x3(qh_split, kn[:, sl], _NT) * scale
            m = jnp.maximum(jnp.where(cnt_c > 0, s_c, -jnp.inf).max(-1, keepdims=True),
                            jnp.where(cnt_n > 0, s_n, -jnp.inf).max(-1, keepdims=True))
            e_c = cnt_c * jnp.exp(jnp.where(cnt_c > 0, s_c - m, -jnp.inf))
            e_n = cnt_n * jnp.exp(jnp.where(cnt_n > 0, s_n - m, -jnp.inf))
            den = e_c.sum(-1, keepdims=True) + e_n.sum(-1, keepdims=True)
            o = _dot_f32x3(e_c, vt_ref[0, b, h], _NT) + _dot_f32x3(e_n, vn[:, sl], _NN)
            o_ref[b, :, sl] = o / den
        return carry
    lax.fori_loop(0, bt, body, 0)


def dil_sample(layer, q, kn, vn, cache_kt, cache_vt, bt=2):
    bsz, n_new, w = q.shape
    n_buf = cache_kt.shape[-1]
    tok = pl.BlockSpec((bt, n_new, w), lambda i: (i, 0, 0))
    buf = pl.BlockSpec((1, bt, DIL_HEADS, HEAD_DIM, n_buf), lambda i: (layer, i, 0, 0, 0))
    return pl.pallas_call(
        _dil_sample_kernel,
        grid=(bsz // bt,),
        in_specs=[tok, tok, tok, buf, buf],
        out_specs=tok,
        out_shape=jax.ShapeDtypeStruct((bsz, n_new, w), F32),
        compiler_params=_cparams(("parallel",)),
        name="dil_sample",
    )(q, kn, vn, cache_kt, cache_vt)


def _swa_sample_kernel(sink_ref, q_ref, kn_ref, vn_ref, kt_ref, vt_ref, o_ref):
    bt, n_new = q_ref.shape[0], q_ref.shape[1]
    n_buf = kt_ref.shape[-1]
    i_c = lax.broadcasted_iota(I32, (n_new, n_buf), 0)
    c_c = lax.broadcasted_iota(I32, (n_new, n_buf), 1)
    mask_c = (c_c >= i_c) & (n_buf + i_c - c_c <= SWA_WINDOW)
    i_n = lax.broadcasted_iota(I32, (n_new, n_new), 0)
    t_n = lax.broadcasted_iota(I32, (n_new, n_new), 1)
    mask_n = t_n <= i_n
    scale = HEAD_DIM ** -0.5
    nt = (((1,), (1,)), ((), ()))
    grp = SWA_HEADS // SWA_KV_HEADS

    def body(b, carry):
        q, kn, vn = q_ref[b], kn_ref[b], vn_ref[b]
        for pos, h in enumerate(SWA_HEAD_ORDER):
            g = h // grp
            sl = slice(pos * HEAD_DIM, (pos + 1) * HEAD_DIM)
            gl = slice(g * HEAD_DIM, (g + 1) * HEAD_DIM)
            qh = q[:, sl].astype(BF16)
            kt = kt_ref[0, b, g].astype(BF16)
            vt = vt_ref[0, b, g].astype(BF16)
            s_c = jnp.where(mask_c, jnp.dot(qh, kt, preferred_element_type=F32) * scale, -jnp.inf)
            s_n = jnp.where(mask_n, lax.dot_general(qh, kn[:, gl].astype(BF16), nt, preferred_element_type=F32) * scale,
                            -jnp.inf)
            sink = sink_ref[h]
            m = jnp.maximum(jnp.maximum(s_c.max(-1, keepdims=True), s_n.max(-1, keepdims=True)), sink)
            e_c = jnp.exp(s_c - m)
            e_n = jnp.exp(s_n - m)
            den = e_c.sum(-1, keepdims=True) + e_n.sum(-1, keepdims=True) + jnp.exp(sink - m)
            o_ref[b, :, sl] = (lax.dot_general((e_c / den).astype(BF16), vt, nt, preferred_element_type=F32)
                               + jnp.dot((e_n / den).astype(BF16), vn[:, gl].astype(BF16), preferred_element_type=F32))
        return carry
    lax.fori_loop(0, bt, body, 0)


def swa_sample(layer, sinks, q, kn, vn, cache_kt, cache_vt, bt=16):
    bsz, n_new, w = q.shape
    n_buf = cache_kt.shape[-1]
    tok = lambda n: pl.BlockSpec((bt, n_new, n), lambda i: (i, 0, 0))
    buf = pl.BlockSpec((1, bt, SWA_KV_HEADS, HEAD_DIM, n_buf), lambda i: (layer, i, 0, 0, 0))
    kv = SWA_KV_HEADS * HEAD_DIM
    return pl.pallas_call(
        _swa_sample_kernel,
        grid=(bsz // bt,),
        in_specs=[pl.BlockSpec(memory_space=pltpu.SMEM), tok(w), tok(kv), tok(kv), buf, buf],
        out_specs=tok(w),
        out_shape=jax.ShapeDtypeStruct((bsz, n_new, w), F32),
        compiler_params=_cparams(("parallel",)),
        name="swa_sample",
    )(sinks, q, kn, vn, cache_kt, cache_vt)


def _mlstm_sample_kernel(q_ref, k_ref, v_ref, kc_ref, om_ref, gc_ref, gr_ref, brow_ref, bcol_ref, nw_ref,
                         c_in, n_in, m_in, o_ref, c_out, n_out, m_out):
    bt, L = q_ref.shape[0], q_ref.shape[1]
    H = MLSTM_HEADS
    r = lax.broadcasted_iota(I32, (L, L), 0)
    c = lax.broadcasted_iota(I32, (L, L), 1)
    causal = c <= r

    def body(b, carry):
        g_col = gc_ref[b] + brow_ref[...]
        g_row = gr_ref[b] + bcol_ref[...]
        for h in range(H):
            sl = slice(h * MLSTM_DK, (h + 1) * MLSTM_DK)
            q, k, v = q_ref[b, :, sl], k_ref[b, :, sl], v_ref[b, :, sl]
            k_col = kc_ref[b, h]
            c_prev, n_prev, m_prev = c_in[0, b, h], n_in[0, b, h], m_in[0, b, h][:, 0:1]
            ig_col, ig_row = g_col[:, h:h + 1], g_row[h:h + 1, :]
            lf_col, lf_row = _log_sigmoid(g_col[:, H + h:H + h + 1]), _log_sigmoid(g_row[H + h:H + h + 1, :])
            b_col = jnp.where(causal, lf_row, 0.0).sum(-1, keepdims=True)
            b_row = jnp.where(r <= c, lf_col, 0.0).sum(0, keepdims=True)

            log_d = jnp.where(causal, b_col - b_row + ig_row, -jnp.inf)
            log_inter = b_col + m_prev
            m_t = jnp.maximum(log_inter, log_d.max(-1, keepdims=True))
            s_qk = lax.dot_general(q.astype(BF16), k.astype(BF16), (((1,), (1,)), ((), ())), preferred_element_type=F32)
            w_ts = jnp.exp(log_d - m_t) * s_qk
            w_inter = jnp.exp(log_inter - m_t)
            num = w_inter * jnp.dot(q.astype(BF16), c_prev.astype(BF16), preferred_element_type=F32)
            round_bf = lambda a: a.astype(BF16).astype(F32)
            w_ts_r = round_bf(w_ts)
            for s in range(L):
                num = num + w_ts_r[:, s:s + 1] * v[s:s + 1, :]
            den = w_ts.sum(-1, keepdims=True) + w_inter * (q * round_bf(n_prev)).sum(-1, keepdims=True)
            hh = num / jnp.maximum(jnp.abs(den), jnp.exp(-m_t))

            mu = hh.mean(-1, keepdims=True)
            var = jnp.mean(jnp.square(hh - mu), -1, keepdims=True)
            hn = (hh - mu) * lax.rsqrt(var + LN_EPS) * nw_ref[:, sl]
            o_ref[b, :, sl] = hn * jax.nn.sigmoid(om_ref[b, :, sl])

            b_last = b_col[L - 1:L, :]
            m_new = jnp.maximum(b_last + m_prev, (b_last - b_row + ig_row).max(-1, keepdims=True))
            w_s = jnp.exp(b_last - b_col + ig_col - m_new)
            w_s_row = jnp.exp(b_last - b_row + ig_row - m_new)
            decay = jnp.exp(b_last + m_prev - m_new)
            c_out[b, h] = decay * c_prev + jnp.dot((k_col * w_s_row).astype(BF16), v.astype(BF16), preferred_element_type=F32)
            n_out[b, h] = decay * n_prev + (round_bf(w_s) * round_bf(k)).sum(0, keepdims=True)
            m_out[b, h] = jnp.broadcast_to(m_new, (1, LANES))
        return carry
    lax.fori_loop(0, bt, body, 0, unroll=2)


def mlstm_sample(layer, q, k, v, k_col, om, gc, gr, bias_row, bias_col, norm_w, c_in, n_in, m_in, bt=8):
    bsz, L, w = q.shape
    H = MLSTM_HEADS
    lead = lambda *tail: pl.BlockSpec((bt,) + tail, lambda i: (i,) + (0,) * len(tail))
    const = lambda shape: pl.BlockSpec(shape, lambda i: (0,) * len(shape))
    st_in = lambda *tail: pl.BlockSpec((1, bt, H) + tail, lambda i: (layer, i, 0) + (0,) * len(tail))
    return pl.pallas_call(
        _mlstm_sample_kernel,
        grid=(bsz // bt,),
        in_specs=[lead(L, w), lead(L, w), lead(L, w), lead(H, MLSTM_DK, L), lead(L, w),
                  lead(L, 8), lead(8, L), const((1, 8)), const((8, 1)), const((1, w)),
                  st_in(MLSTM_DK, MLSTM_DV), st_in(1, MLSTM_DK), st_in(1, LANES)],
        out_specs=[lead(L, w), lead(H, MLSTM_DK, MLSTM_DV), lead(H, 1, MLSTM_DK), lead(H, 1, LANES)],
        out_shape=[jax.ShapeDtypeStruct((bsz, L, w), F32),
                   jax.ShapeDtypeStruct((bsz, H, MLSTM_DK, MLSTM_DV), F32),
                   jax.ShapeDtypeStruct((bsz, H, 1, MLSTM_DK), F32),
                   jax.ShapeDtypeStruct((bsz, H, 1, LANES), F32)],
        compiler_params=_cparams(("parallel",)),
        name="mlstm_sample",
    )(q, k, v, k_col, om, gc, gr, bias_row, bias_col, norm_w, c_in, n_in, m_in)


def _cache_roll_kernel(k_ref, v_ref, kn_ref, vn_ref, ko_ref, vo_ref):
    shape = k_ref.shape
    n_buf, n_new = shape[-1], kn_ref.shape[-1]
    rows = shape[1] * shape[2] * shape[3]
    lane = lax.broadcasted_iota(I32, (rows, LANES), 1)
    for src, new, dst in ((k_ref, kn_ref, ko_ref), (v_ref, vn_ref, vo_ref)):
        rolled = pltpu.roll(src[0].reshape(rows, n_buf), n_buf - n_new, 1)
        tail = rolled[:, n_buf - LANES:]
        fresh = new[0].reshape(rows, n_new)
        for t in range(n_new):
            tail = jnp.where(lane == LANES - n_new + t, fresh[:, t:t + 1], tail)
        dst[0] = rolled.reshape(shape[1:])
        dst[0, :, :, :, n_buf - LANES:] = tail.reshape(shape[1:4] + (LANES,))


def cache_roll(cache_kt, cache_vt, new_kt, new_vt, bt):
    depth, bsz, h, d, n_buf = cache_kt.shape
    n_new = new_kt.shape[-1]
    blk = lambda n: pl.BlockSpec((1, bt, h, d, n), lambda l, i: (l, i, 0, 0, 0))
    return pl.pallas_call(
        _cache_roll_kernel,
        grid=(depth, bsz // bt),
        in_specs=[blk(n_buf), blk(n_buf), blk(n_new), blk(n_new)],
        out_specs=[blk(n_buf), blk(n_buf)],
        out_shape=[jax.ShapeDtypeStruct(cache_kt.shape, F32)] * 2,
        compiler_params=_cparams(("parallel", "parallel")),
        name="cache_roll",
    )(cache_kt, cache_vt, new_kt, new_vt)


PAST_LEN = 8192


def kernel(x_prompt, x_sample, cache_swa_k, cache_swa_v, cache_dil_k, cache_dil_v, state_mlstm_c, state_mlstm_n, state_mlstm_m, w_in, w_out, attn_sinks, mlstm_b_i, mlstm_b_f, mlstm_norm_w, ln1_g, ln1_b, ln2_g, ln2_b, router_w, router_b, moe_w_gate, moe_w_up, moe_w_down):
    bsz, t_len, d = x_prompt.shape
    sbsz, s_len, _ = x_sample.shape
    n_p, n_s = bsz * t_len, sbsz * s_len
    n_tok = n_p + n_s
    depth = w_in.shape[0]

    x = jnp.concatenate([x_prompt.reshape(n_p, d), x_sample.reshape(n_s, d)], axis=0)
    pos = jnp.concatenate([jnp.tile(jnp.arange(t_len, dtype=I32), bsz),
                           jnp.tile(PAST_LEN + jnp.arange(s_len, dtype=I32), sbsz)])
    cos, sa, sb = rope_tables(pos)

    to_lanes = lambda c: jnp.transpose(c, (0, 1, 3, 4, 2))
    from_lanes = lambda c: jnp.transpose(c, (0, 1, 4, 2, 3))
    swa_kt, swa_vt, dil_kt, dil_vt = map(to_lanes, (cache_swa_k, cache_swa_v, cache_dil_k, cache_dil_v))
    n_in = state_mlstm_n[:, :, :, None, :]
    m_in = jnp.broadcast_to(state_mlstm_m[..., None, None], state_mlstm_m.shape + (1, LANES))
    rw_t, rb_col = router_w.T, router_b[:, None]

    order = jnp.array(SWA_HEAD_ORDER)
    n_qa = SWA_HEADS * HEAD_DIM
    w_in_b = w_in.astype(BF16)
    w_qa = w_in_b[:, :, :n_qa].reshape(depth, d, SWA_HEADS, HEAD_DIM)[:, :, order].reshape(depth, d, n_qa)
    w_main = jnp.concatenate([w_qa, w_in_b[:, :, n_qa:_MAIN_COLS]], axis=2)
    w_gates = w_in_b[:, :, _MAIN_COLS:]
    w_gates_t = jnp.transpose(w_gates, (0, 2, 1))
    w_out_b = w_out.astype(BF16)
    w_oa = w_out_b[:, :n_qa].reshape(depth, SWA_HEADS, HEAD_DIM, d)[:, order].reshape(depth, n_qa, d)
    w_out_b = jnp.concatenate([w_oa, w_out_b[:, n_qa:]], axis=1)

    p_states, s_states, new_cols = [], [], []
    for l in range(depth):
        qa, ka, va, qd, kd, vd, qm, km, vm, om, gc, gr = in_proj(x, w_main[l], w_gates[l], w_gates_t[l], cos, sa, sb)
        bias = jnp.concatenate([mlstm_b_i[l], mlstm_b_f[l]])
        bias_row, bias_col = bias[None, :], bias[:, None]
        norm_w = mlstm_norm_w[l][None, :]

        oa_p = swa_prompt(attn_sinks[l], qa, ka, va, bsz, t_len)
        od_p = dil_prompt(qd, kd, vd, bsz, t_len)
        om_p, pc, pn, pm = mlstm_prompt(qm, km, vm, om, gc, gr, bias_row, bias_col, norm_w, bsz, t_len)
        om_p = om_p.reshape(n_p, om_p.shape[2])

        s3 = lambda a: a[n_p:].astype(F32).reshape(sbsz, s_len, a.shape[1])
        qa_s, ka_s, va_s, qd_s, kd_s, vd_s, qm_s, km_s, vm_s, om_s, gc_s = map(
            s3, (qa, ka, va, qd, kd, vd, qm, km, vm, om, gc))
        gr_s = jnp.transpose(gc_s, (0, 2, 1))
        col = lambda a: jnp.transpose(a.reshape(sbsz, s_len, MLSTM_HEADS, MLSTM_DK), (0, 2, 3, 1))
        oa_s = swa_sample(l, attn_sinks[l], qa_s, ka_s, va_s, swa_kt, swa_vt)
        od_s = dil_sample(l, qd_s, kd_s, vd_s, dil_kt, dil_vt)
        omx_s, sc, sn, sm = mlstm_sample(l, qm_s, km_s, vm_s, col(km_s), om_s, gc_s, gr_s,
                                         bias_row, bias_col, norm_w, state_mlstm_c, n_in, m_in)
        flat = lambda a: a.reshape(n_s, a.shape[2]).astype(BF16)

        x1t, e_ids, gates = out_proj(oa_p, od_p, om_p, flat(oa_s), flat(od_s), flat(omx_s), x,
                                     w_out_b[l], ln1_g[l][None, :], ln1_b[l][None, :], rw_t, rb_col)
        y = moe_experts(l, x1t, *moe_plan(e_ids, n_tok), moe_w_gate, moe_w_up, moe_w_down)
        x = moe_combine_ln2(x1t, y, gates.T, ln2_g[l][None, :], ln2_b[l][None, :])

        tail = lambda a, h, keep: jnp.stack([a[(b + 1) * t_len - keep:(b + 1) * t_len] for b in range(bsz)]
                                            ).reshape(bsz, keep, h, HEAD_DIM)
        swa_keep, dil_keep = min(SWA_WINDOW, t_len), min(DIL_WINDOW_MAX, t_len)
        p_states.append((tail(ka, SWA_KV_HEADS, swa_keep), tail(va, SWA_KV_HEADS, swa_keep),
                         tail(kd, DIL_HEADS, dil_keep), tail(vd, DIL_HEADS, dil_keep),
                         pc, pn[:, :, 0, :], pm[:, :, 0, 0]))
        s_states.append((sc, sn[:, :, 0, :], sm[:, :, 0, 0]))
        new_t = lambda a, h: jnp.transpose(a.reshape(sbsz, s_len, h, HEAD_DIM), (0, 2, 3, 1))
        new_cols.append((new_t(ka_s, SWA_KV_HEADS), new_t(va_s, SWA_KV_HEADS), new_t(kd_s, DIL_HEADS), new_t(vd_s, DIL_HEADS)))

    p_swa_k, p_swa_v, p_dil_k, p_dil_v, p_c, p_n, p_m = [jnp.stack(a) for a in zip(*p_states)]
    s_c, s_n, s_m = [jnp.stack(a) for a in zip(*s_states)]
    nk_swa, nv_swa, nk_dil, nv_dil = [jnp.stack(a) for a in zip(*new_cols)]
    s_swa_kt, s_swa_vt = cache_roll(swa_kt, swa_vt, nk_swa, nv_swa, bt=16)
    s_dil_kt, s_dil_vt = cache_roll(dil_kt, dil_vt, nk_dil, nv_dil, bt=1)
    return (x[:n_p].reshape(bsz, t_len, d), x[n_p:].reshape(sbsz, s_len, d),
            p_swa_k, p_swa_v, p_dil_k, p_dil_v, p_c, p_n, p_m,
            from_lanes(s_swa_kt), from_lanes(s_swa_vt), from_lanes(s_dil_kt), from_lanes(s_dil_vt), s_c, s_n, s_m)
```
